```python
import math
import functools
import jax
import jax.numpy as jnp
from jax import lax
import numpy as np

D_MODEL = 2048
BATCH = 8
SEQ = 2048
DEPTH = 1
DEC_BATCH = 128
DEC_SEQ = 8
PAST_LEN = 2048
PAGE_SIZE = 128

A_HEADS = 8
A_HEAD_DIM = 128
A_WIDTH = A_HEADS * A_HEAD_DIM
MOBA_BLOCK = 256
MOBA_TOPK = 3
Q_CHUNK = 16
REL_BUCKETS = 32
REL_MAX_DIST = 128
G_HEADS = 8
G_KEY_DIM = 128
G_VAL_DIM = 128
G_KEY_WIDTH = G_HEADS * G_KEY_DIM
G_VAL_WIDTH = G_HEADS * G_VAL_DIM
CONV_W = 4
CONV_CH = 2 * G_KEY_WIDTH + G_VAL_WIDTH
DELTA_CHUNK = 64
N_EXPERTS = 64
TOP_K = 6
N_GROUPS = 8
TOPK_GROUPS = 4
D_EXPERT = 512
D_SHARED = 512
ROUTED_SCALE = 2.5
EXPERT_ROWS = 256

EPS = 1e-6
N_ADA = 6
IN_SIZES = (A_WIDTH, A_WIDTH, A_WIDTH, CONV_CH, G_VAL_WIDTH, G_HEADS, G_HEADS, D_MODEL, D_MODEL)
IN_OFFSETS = tuple(int(o) for o in np.cumsum(IN_SIZES)[:-1])
IN_TOTAL = int(sum(IN_SIZES))

kernel_name = 'hybrid_moba_gdn_moe_decoder_step'


def rmsnorm(x, g):
    xf = x.astype(jnp.float32)
    y = xf * lax.rsqrt(jnp.mean(xf * xf, axis=-1, keepdims=True) + EPS)
    return (y * g.astype(jnp.float32)).astype(x.dtype)


def l2norm(x):
    xf = x.astype(jnp.float32)
    return (xf * lax.rsqrt(jnp.sum(xf * xf, axis=-1, keepdims=True) + EPS)).astype(x.dtype)


def split_heads(t, n_heads):
    return t.reshape(t.shape[:-1] + (n_heads, t.shape[-1] // n_heads))


def rel_bucket(dist):
    n = jnp.maximum(dist, 0)
    max_exact = REL_BUCKETS // 2
    nf = jnp.maximum(n, 1).astype(jnp.float32)
    large = max_exact + (jnp.log(nf / max_exact) / math.log(REL_MAX_DIST / max_exact)
                         * (REL_BUCKETS - max_exact)).astype(jnp.int32)
    return jnp.where(n < max_exact, n, jnp.minimum(large, REL_BUCKETS - 1))


def swiglu(x, w_g, w_u, w_d):
    return (jax.nn.silu(x @ w_g) * (x @ w_u)) @ w_d


def moba_prompt(q, k, v, rel_bias):
    B, S, H, Dh = q.shape
    f32 = jnp.float32
    scale = Dh ** -0.5
    nb = -(-S // MOBA_BLOCK)
    pad = nb * MOBA_BLOCK - S
    k_blk = jnp.pad(k, ((0, 0), (0, pad), (0, 0), (0, 0))).reshape(B, nb, MOBA_BLOCK, H, Dh)
    v_blk = jnp.pad(v, ((0, 0), (0, pad), (0, 0), (0, 0))).reshape(B, nb, MOBA_BLOCK, H, Dh)
    n_sel = min(MOBA_TOPK, nb - 1)
    n_chunks = S // Q_CHUNK
    q_c = jnp.moveaxis(q.reshape(B, n_chunks, Q_CHUNK, H, Dh), 1, 0)
    starts = jnp.arange(n_chunks, dtype=jnp.int32) * Q_CHUNK
    blk_ar = jnp.arange(MOBA_BLOCK, dtype=jnp.int32)
    b_ix = jnp.arange(B)[:, None, None, None]
    h_ix = jnp.arange(H)[None, None, :, None]
    n_s = n_sel * MOBA_BLOCK
    if n_sel > 0:
        k_mean = jnp.mean(k_blk.astype(f32), axis=2)
        q_blk = jnp.arange(S, dtype=jnp.int32) // MOBA_BLOCK
        gate = jnp.einsum('bshd,bnhd->bshn', q.astype(f32), k_mean)
        fully_past = jnp.arange(nb)[None, :] < q_blk[:, None]
        gate = jnp.where(fully_past[None, :, None, :], gate, -jnp.inf)
        sel = lax.top_k(gate, n_sel)[1]
        sel_c = jnp.moveaxis(sel.reshape(B, n_chunks, Q_CHUNK, H, n_sel), 1, 0)
        k_bh = jnp.transpose(k_blk, (0, 3, 1, 2, 4))
        v_bh = jnp.transpose(v_blk, (0, 3, 1, 2, 4))
    else:
        sel_c = jnp.zeros((n_chunks, B, Q_CHUNK, H, 0), jnp.int32)

    def chunk_attend(xs):
        qc, selc, q0 = xs
        own = q0 // MOBA_BLOCK
        qpos = q0 + jnp.arange(Q_CHUNK, dtype=jnp.int32)
        k_loc = lax.dynamic_index_in_dim(k_blk, own, axis=1, keepdims=False)
        v_loc = lax.dynamic_index_in_dim(v_blk, own, axis=1, keepdims=False)
        kpos_loc = own * MOBA_BLOCK + blk_ar
        bias_loc = jnp.transpose(rel_bias[rel_bucket(qpos[:, None] - kpos_loc[None, :])], (0, 2, 1))
        s_loc = jnp.einsum('bqhd,bkhd->bqhk', qc, k_loc).astype(f32) * scale + bias_loc.astype(f32)[None]
        s_loc = jnp.where((kpos_loc[None, :] <= qpos[:, None])[None, :, None, :], s_loc, -jnp.inf)
        if n_sel == 0:
            p = jax.nn.softmax(s_loc, axis=-1).astype(v.dtype)
            return jnp.einsum('bqhk,bkhd->bqhd', p, v_loc)
        k_sel = k_bh[b_ix, h_ix, selc].reshape(B, Q_CHUNK, H, n_s, Dh)
        v_sel = v_bh[b_ix, h_ix, selc].reshape(B, Q_CHUNK, H, n_s, Dh)
        kpos_sel = (selc[..., None] * MOBA_BLOCK + blk_ar).reshape(B, Q_CHUNK, H, n_s)
        bias_sel = rel_bias[rel_bucket(qpos[None, :, None, None] - kpos_sel), h_ix]
        s_sel = jnp.einsum('bqhd,bqhkd->bqhk', qc, k_sel).astype(f32) * scale + bias_sel.astype(f32)
        s_sel = jnp.where(jnp.repeat(selc < own, MOBA_BLOCK, axis=-1), s_sel, -jnp.inf)
        p = jax.nn.softmax(jnp.concatenate([s_sel, s_loc], axis=-1), axis=-1).astype(v.dtype)
        return (jnp.einsum('bqhk,bqhkd->bqhd', p[..., :n_s], v_sel)
                + jnp.einsum('bqhk,bkhd->bqhd', p[..., n_s:], v_loc))

    o = lax.map(chunk_attend, (q_c, sel_c, starts))
    return jnp.moveaxis(o, 0, 1).reshape(B, S, H, Dh)


def moba_sample(q, k_new, v_new, cache_k, cache_v, layer, page_table, rel_bias):
    DB, T, H, Dh = q.shape
    f32 = jnp.float32
    scale = Dh ** -0.5
    n_pages = PAST_LEN // PAGE_SIZE
    ppb = MOBA_BLOCK // PAGE_SIZE
    n_full = PAST_LEN // MOBA_BLOCK
    local_start = n_full * MOBA_BLOCK
    n_loc_past = PAST_LEN - local_start
    n_sel = min(MOBA_TOPK, n_full)
    n_s = n_sel * MOBA_BLOCK
    qpos = PAST_LEN + jnp.arange(T, dtype=jnp.int32)
    q_blk = qpos // MOBA_BLOCK
    if n_loc_past > 0:
        loc_pages = page_table[:, local_start // PAGE_SIZE:n_pages]
        k_loc = jnp.concatenate([cache_k[layer, loc_pages].reshape(DB, n_loc_past, H, Dh).astype(k_new.dtype), k_new], axis=1)
        v_loc = jnp.concatenate([cache_v[layer, loc_pages].reshape(DB, n_loc_past, H, Dh).astype(v_new.dtype), v_new], axis=1)
    else:
        k_loc, v_loc = k_new, v_new
    kpos_loc = jnp.concatenate([jnp.arange(local_start, PAST_LEN, dtype=jnp.int32), qpos])
    valid_loc = (kpos_loc[None, :] <= qpos[:, None]) & (kpos_loc[None, :] // MOBA_BLOCK == q_blk[:, None])
    bias_loc = jnp.transpose(rel_bias[rel_bucket(qpos[:, None] - kpos_loc[None, :])], (0, 2, 1))
    s_loc = jnp.einsum('bqhd,bkhd->bqhk', q, k_loc).astype(f32) * scale + bias_loc.astype(f32)[None]
    s_loc = jnp.where(valid_loc[None, :, None, :], s_loc, -jnp.inf)
    if n_sel == 0:
        p = jax.nn.softmax(s_loc, axis=-1).astype(v_new.dtype)
        return jnp.einsum('bqhk,bkhd->bqhd', p, v_loc)
    full_pages = page_table[:, :n_full * ppb]
    k_mean = jnp.mean(cache_k[layer, full_pages].astype(f32).reshape(DB, n_full, MOBA_BLOCK, H, Dh), axis=2)
    gate = jnp.einsum('bqhd,bnhd->bqhn', q.astype(f32), k_mean)
    sel = lax.top_k(gate, n_sel)[1]
    phys = page_table[jnp.arange(DB)[:, None, None, None, None], sel[..., None] * ppb + jnp.arange(ppb)]
    row_ix = jnp.arange(PAGE_SIZE)
    h_ix = jnp.arange(H)[None, :, None, None, None]
    blk_ar = jnp.arange(MOBA_BLOCK, dtype=jnp.int32)

    def token_attend(xs):
        q_t, phys_t, sel_t, sloc_t, qpos_t = xs
        pidx = phys_t[..., None]
        k_sel = cache_k[layer, pidx, row_ix, h_ix].reshape(DB, H, n_s, Dh).astype(q_t.dtype)
        v_sel = cache_v[layer, pidx, row_ix, h_ix].reshape(DB, H, n_s, Dh).astype(v_new.dtype)
        kpos = (sel_t[..., None] * MOBA_BLOCK + blk_ar).reshape(DB, H, n_s)
        bias = rel_bias[rel_bucket(qpos_t - kpos), jnp.arange(H)[None, :, None]]
        s_sel = jnp.einsum('bhd,bhkd->bhk', q_t, k_sel).astype(f32) * scale + bias.astype(f32)
        s_sel = jnp.where(jnp.repeat(sel_t < qpos_t // MOBA_BLOCK, MOBA_BLOCK, axis=-1), s_sel, -jnp.inf)
        p = jax.nn.softmax(jnp.concatenate([s_sel, sloc_t], axis=-1), axis=-1).astype(v_new.dtype)
        return (jnp.einsum('bhk,bhkd->bhd', p[..., :n_s], v_sel)
                + jnp.einsum('bhk,bkhd->bhd', p[..., n_s:], v_loc))

    o = lax.map(token_attend, (jnp.moveaxis(q, 1, 0), jnp.moveaxis(phys, 1, 0), jnp.moveaxis(sel, 1, 0),
                               jnp.moveaxis(s_loc, 1, 0), qpos))
    return jnp.moveaxis(o, 0, 1)


def short_conv(x_new, buf, w):
    xp = jnp.concatenate([buf.astype(x_new.dtype), x_new], axis=1)
    y = lax.conv_general_dilated(xp, w[:, None, :].astype(xp.dtype), (1,), 'VALID',
                                 dimension_numbers=('NWC', 'WIO', 'NWC'), feature_group_count=xp.shape[-1])
    return jax.nn.silu(y), xp[:, -(CONV_W - 1):]


def gated_delta_rule(q, k, v, g, beta, s0):
    B, L, H, Dk = q.shape
    Dv = v.shape[-1]
    C = DELTA_CHUNK
    n = -(-L // C)
    pad = n * C - L
    f32 = jnp.float32

    def prep(a):
        a = jnp.pad(a.astype(f32), [(0, 0), (0, pad)] + [(0, 0)] * (a.ndim - 2))
        a = a.reshape((B, n, C) + a.shape[2:])
        return jnp.moveaxis(jnp.moveaxis(a, 1, 0), 3, 2)

    qc, kc, vc, gc, bc = prep(q), prep(k), prep(v), prep(g), prep(beta)
    qc = qc * Dk ** -0.5
    gc = jnp.cumsum(gc, axis=-1)
    incl = jnp.tril(jnp.ones((C, C), bool))
    strict = jnp.tril(jnp.ones((C, C), bool), -1)
    dmask = jnp.exp(jnp.where(incl, gc[..., :, None] - gc[..., None, :], -jnp.inf))
    kb = kc * bc[..., None]
    a_mat = jnp.where(strict, jnp.einsum('nbhik,nbhjk->nbhij', kb, kc) * dmask, 0.0)
    t_mat = a_mat + jnp.eye(C, dtype=f32)
    u = lax.linalg.triangular_solve(t_mat, vc * bc[..., None], left_side=True, lower=True, unit_diagonal=True)
    w = lax.linalg.triangular_solve(t_mat, kb * jnp.exp(gc)[..., None], left_side=True, lower=True, unit_diagonal=True)

    def chunk_step(S, xs):
        q_i, k_i, u_i, w_i, g_i, dm_i = xs
        v_new = u_i - jnp.einsum('bhck,bhkv->bhcv', w_i, S)
        attn = jnp.einsum('bhik,bhjk->bhij', q_i, k_i) * dm_i
        o = (jnp.einsum('bhck,bhkv->bhcv', q_i * jnp.exp(g_i)[..., None], S)
             + jnp.einsum('bhij,bhjv->bhiv', attn, v_new))
        g_last = g_i[..., -1:]
        S = S * jnp.exp(g_last)[..., None] + jnp.einsum('bhck,bhcv->bhkv', k_i * jnp.exp(g_last - g_i)[..., None], v_new)
        return S, o

    s_fin, o = lax.scan(chunk_step, s0.astype(f32), (qc, kc, u, w, gc, dmask))
    o = jnp.transpose(o, (1, 0, 3, 2, 4)).reshape(B, n * C, H, Dv)[:, :L]
    return o.astype(v.dtype), s_fin.astype(s0.dtype)


def moe_ffn(h, router_w, router_bias, w_gate_e, w_up_e, w_down_e, w_gate_s, w_up_s, w_down_s):
    N, D = h.shape
    f32 = jnp.float32
    scores = jax.nn.sigmoid((h @ router_w).astype(f32))
    biased = scores + router_bias.astype(f32)
    per_group = N_EXPERTS // N_GROUPS
    group_score = jnp.sum(lax.top_k(biased.reshape(N, N_GROUPS, per_group), 2)[0], axis=-1)
    group_keep = jnp.sum(jax.nn.one_hot(lax.top_k(group_score, TOPK_GROUPS)[1], N_GROUPS, dtype=f32), axis=1) > 0
    expert_keep = jnp.repeat(group_keep, per_group, axis=1)
    top_e = lax.top_k(jnp.where(expert_keep, biased, -jnp.inf), TOP_K)[1]
    top_w = jnp.take_along_axis(scores, top_e, axis=1)
    top_w = top_w / jnp.sum(top_w, axis=-1, keepdims=True) * ROUTED_SCALE
    nk = N * TOP_K
    flat_e = top_e.reshape(nk)
    flat_tok = jnp.repeat(jnp.arange(N, dtype=jnp.int32), TOP_K)
    flat_w = top_w.reshape(nk)
    order = jnp.argsort(flat_e, stable=True)
    e_sorted = flat_e[order]
    counts = jnp.bincount(flat_e, length=N_EXPERTS)
    starts = jnp.cumsum(counts) - counts
    padded = (counts + EXPERT_ROWS - 1) // EXPERT_ROWS * EXPERT_ROWS
    pad_ends = jnp.cumsum(padded)
    pad_starts = pad_ends - padded
    dest = pad_starts[e_sorted] + jnp.arange(nk, dtype=jnp.int32) - starts[e_sorted]
    n_blocks = -(-(nk + N_EXPERTS * (EXPERT_ROWS - 1)) // EXPERT_ROWS)
    n_rows = n_blocks * EXPERT_ROWS
    row_tok = jnp.full((n_rows,), N, jnp.int32).at[dest].set(flat_tok[order])
    row_w = jnp.zeros((n_rows,), f32).at[dest].set(flat_w[order])
    blk_e = jnp.minimum(jnp.searchsorted(pad_ends, jnp.arange(n_blocks, dtype=jnp.int32) * EXPERT_ROWS, side='right'),
                        N_EXPERTS - 1)
    h_pad = jnp.concatenate([h, jnp.zeros((1, D), h.dtype)], axis=0)

    def expert_block(xs):
        tok, wt, e = xs
        y = swiglu(h_pad[tok], w_gate_e[e], w_up_e[e], w_down_e[e])
        return y.astype(f32) * wt[:, None]

    y_rows = lax.map(expert_block, (row_tok.reshape(n_blocks, EXPERT_ROWS), row_w.reshape(n_blocks, EXPERT_ROWS), blk_e))
    routed = jax.ops.segment_sum(y_rows.reshape(n_rows, D), row_tok, num_segments=N + 1)[:N]
    return (routed + swiglu(h, w_gate_s, w_up_s, w_down_s).astype(f32)).astype(h.dtype)


def decoder_layer(x, c, attend, s0, buf0, w_ada, b_ada, norm_mix, w_in, conv_w, a_log, dt_bias, delta_norm,
                  w_branch_a, w_branch_b, w_out, norm_ffn, router_w, router_bias,
                  w_gate_e, w_up_e, w_down_e, w_gate_s, w_up_s, w_down_s):
    B, L, D = x.shape
    f32 = jnp.float32
    mod = jax.nn.silu(c) @ w_ada + b_ada
    sh_m, sc_m, g_m, sh_f, sc_f, g_f = jnp.split(mod[:, None, :], N_ADA, axis=-1)
    h = rmsnorm(x, norm_mix) * (1 + sc_m) + sh_m
    a_q, a_k, a_v, g_qkv, g_z, g_beta, g_a, gate_a, gate_b = jnp.split(h @ w_in, IN_OFFSETS, axis=-1)
    k_h = split_heads(a_k, A_HEADS)
    v_h = split_heads(a_v, A_HEADS)
    y_a = attend(split_heads(a_q, A_HEADS), k_h, v_h).reshape(B, L, A_WIDTH)
    qkv, buf_new = short_conv(g_qkv, buf0, conv_w)
    g_q, g_k, g_v = jnp.split(qkv, [G_KEY_WIDTH, 2 * G_KEY_WIDTH], axis=-1)
    beta = jax.nn.sigmoid(g_beta.astype(f32))
    log_decay = -jnp.exp(a_log.astype(f32)) * jax.nn.softplus(g_a.astype(f32) + dt_bias.astype(f32))
    o_d, s_new = gated_delta_rule(l2norm(split_heads(g_q, G_HEADS)), l2norm(split_heads(g_k, G_HEADS)),
                                  split_heads(g_v, G_HEADS), log_decay, beta, s0)
    y_b = (rmsnorm(o_d, delta_norm) * jax.nn.silu(split_heads(g_z, G_HEADS))).reshape(B, L, G_VAL_WIDTH)
    merged = jax.nn.sigmoid(gate_a) * (y_a @ w_branch_a) + jax.nn.sigmoid(gate_b) * (y_b @ w_branch_b)
    x = x + g_m * (merged @ w_out)
    h2 = rmsnorm(x, norm_ffn) * (1 + sc_f) + sh_f
    y_ffn = moe_ffn(h2.reshape(B * L, D), router_w, router_bias, w_gate_e, w_up_e, w_down_e,
                    w_gate_s, w_up_s, w_down_s).reshape(B, L, D)
    x = x + g_f * y_ffn
    return x, k_h, v_h, s_new, buf_new


def setup_inputs(seed: int = 0) -> dict:
    key = jax.random.key(seed)
    ks = jax.random.split(key, 32)
    f32 = jnp.float32

    def nrm(k, shape, s):
        return jax.random.normal(k, shape, f32) * s

    n_pages = PAST_LEN // PAGE_SIZE
    n_used = DEC_BATCH * n_pages
    n_phys = n_used + n_used // 4
    page_table = jax.random.permutation(ks[6], n_phys)[:n_used].reshape(DEC_BATCH, n_pages).astype(jnp.int32)
    dt = jnp.exp(jax.random.uniform(ks[15], (DEPTH, G_HEADS), f32, math.log(1e-3), math.log(1e-1)))
    return {
        'x_prompt': nrm(ks[0], (BATCH, SEQ, D_MODEL), 1.0),
        'x_sample': nrm(ks[1], (DEC_BATCH, DEC_SEQ, D_MODEL), 1.0),
        'cache_k': nrm(ks[2], (DEPTH, n_phys, PAGE_SIZE, A_HEADS, A_HEAD_DIM), 1.0),
        'cache_v': nrm(ks[3], (DEPTH, n_phys, PAGE_SIZE, A_HEADS, A_HEAD_DIM), 1.0),
        'state_delta': nrm(ks[4], (DEPTH, DEC_BATCH, G_HEADS, G_KEY_DIM, G_VAL_DIM), G_KEY_DIM ** -0.5),
        'state_conv': nrm(ks[5], (DEPTH, DEC_BATCH, CONV_W - 1, CONV_CH), 1.0),
        'page_table': page_table,
        'c_prompt': nrm(ks[7], (BATCH, D_MODEL), 1.0),
        'c_sample': nrm(ks[8], (DEC_BATCH, D_MODEL), 1.0),
        'w_ada': nrm(ks[9], (DEPTH, D_MODEL, N_ADA * D_MODEL), 0.5 * D_MODEL ** -0.5),
        'b_ada': nrm(ks[10], (DEPTH, N_ADA * D_MODEL), 0.01),
        'norm_mix': 1.0 + nrm(ks[11], (DEPTH, D_MODEL), 0.01),
        'w_in': nrm(ks[12], (DEPTH, D_MODEL, IN_TOTAL), D_MODEL ** -0.5),
        'conv_w': nrm(ks[13], (DEPTH, CONV_W, CONV_CH), CONV_W ** -0.5),
        'a_log': jnp.log(jax.random.uniform(ks[14], (DEPTH, G_HEADS), f32, 1.0, 16.0)),
        'dt_bias': dt + jnp.log(-jnp.expm1(-dt)),
        'delta_norm': 1.0 + nrm(ks[16], (DEPTH, G_VAL_DIM), 0.01),
        'w_branch_a': nrm(ks[17], (DEPTH, A_WIDTH, D_MODEL), A_WIDTH ** -0.5),
        'w_branch_b': nrm(ks[18], (DEPTH, G_VAL_WIDTH, D_MODEL), G_VAL_WIDTH ** -0.5),
        'w_out': nrm(ks[19], (DEPTH, D_MODEL, D_MODEL), D_MODEL ** -0.5),
        'rel_bias': nrm(ks[20], (REL_BUCKETS, A_HEADS), 0.5),
        'norm_ffn': 1.0 + nrm(ks[21], (DEPTH, D_MODEL), 0.01),
        'router_w': nrm(ks[22], (DEPTH, D_MODEL, N_EXPERTS), D_MODEL ** -0.5),
        'router_bias': nrm(ks[23], (DEPTH, N_EXPERTS), 0.01),
        'w_gate_e': nrm(ks[24], (DEPTH, N_EXPERTS, D_MODEL, D_EXPERT), D_MODEL ** -0.5),
        'w_up_e': nrm(ks[25], (DEPTH, N_EXPERTS, D_MODEL, D_EXPERT), D_MODEL ** -0.5),
        'w_down_e': nrm(ks[26], (DEPTH, N_EXPERTS, D_EXPERT, D_MODEL), D_EXPERT ** -0.5),
        'w_gate_s': nrm(ks[27], (DEPTH, D_MODEL, D_SHARED), D_MODEL ** -0.5),
        'w_up_s': nrm(ks[28], (DEPTH, D_MODEL, D_SHARED), D_MODEL ** -0.5),
        'w_down_s': nrm(ks[29], (DEPTH, D_SHARED, D_MODEL), D_SHARED ** -0.5),
        'norm_final': 1.0 + nrm(ks[30], (D_MODEL,), 0.01),
    }


def reference(x_prompt, x_sample, cache_k, cache_v, state_delta, state_conv, page_table, c_prompt, c_sample,
              w_ada, b_ada, norm_mix, w_in, conv_w, a_log, dt_bias, delta_norm, w_branch_a, w_branch_b, w_out,
              rel_bias, norm_ffn, router_w, router_bias, w_gate_e, w_up_e, w_down_e, w_gate_s, w_up_s, w_down_s,
              norm_final):
    n_prompt = x_prompt.shape[0]
    xp, xs = x_prompt, x_sample
    k_p, v_p, k_s, v_s, d_p, cv_p, d_s, cv_s = [], [], [], [], [], [], [], []
    for l in range(DEPTH):
        lw = (w_ada[l], b_ada[l], norm_mix[l], w_in[l], conv_w[l], a_log[l], dt_bias[l], delta_norm[l],
              w_branch_a[l], w_branch_b[l], w_out[l], norm_ffn[l], router_w[l], router_bias[l],
              w_gate_e[l], w_up_e[l], w_down_e[l], w_gate_s[l], w_up_s[l], w_down_s[l])
        attend_p = functools.partial(moba_prompt, rel_bias=rel_bias)
        attend_s = functools.partial(moba_sample, cache_k=cache_k, cache_v=cache_v, layer=l,
                                     page_table=page_table, rel_bias=rel_bias)
        s0 = jnp.zeros((n_prompt, G_HEADS, G_KEY_DIM, G_VAL_DIM), x_prompt.dtype)
        b0 = jnp.zeros((n_prompt, CONV_W - 1, CONV_CH), x_prompt.dtype)
        xp, kp_l, vp_l, dp_l, cp_l = decoder_layer(xp, c_prompt, attend_p, s0, b0, *lw)
        xs, ks_l, vs_l, ds_l, cs_l = decoder_layer(xs, c_sample, attend_s, state_delta[l], state_conv[l], *lw)
        k_p.append(kp_l)
        v_p.append(vp_l)
        d_p.append(dp_l)
        cv_p.append(cp_l)
        k_s.append(ks_l)
        v_s.append(vs_l)
        d_s.append(ds_l)
        cv_s.append(cs_l)
    y_prompt = rmsnorm(xp, norm_final)
    y_sample = rmsnorm(xs, norm_final)
    return (y_prompt, y_sample, jnp.stack(k_p), jnp.stack(v_p), jnp.stack(k_s), jnp.stack(v_s),
            jnp.stack(d_p), jnp.stack(cv_p), jnp.stack(d_s), jnp.stack(cv_s))
```

```python
import functools
import math

import numpy as np
import jax
import jax.numpy as jnp
from jax import lax
from jax.experimental import pallas as pl
from jax.experimental.pallas import tpu as pltpu

F32 = jnp.float32
BF16 = jnp.bfloat16
I32 = jnp.int32
EPS = 1e-6
NEG_INF = float("-inf")

A_HEADS = 8
G_HEADS = 8
MOBA_BLOCK = 256
MOBA_TOPK = 3
REL_BUCKETS = 32
REL_MAX_DIST = 128
CONV_W = 4
N_GROUPS = 8
TOPK_GROUPS = 4
TOP_K = 6
ROUTED_SCALE = 2.5
N_ADA = 6
HEAD_DIM = 128

LANES = 128
SUBLANES = 8
VMEM_LIMIT = 56 * 1024 * 1024
EXPERT_ROWS = 256
DELTA_CHUNK = 128


def _cparams(sem, vmem=VMEM_LIMIT):
    return pltpu.CompilerParams(dimension_semantics=sem, vmem_limit_bytes=vmem)


def _dot(a, b):
    return jnp.dot(a.astype(BF16), b.astype(BF16), preferred_element_type=F32)


def _dot_nt(a, b):
    return lax.dot_general(a.astype(BF16), b.astype(BF16), (((1,), (1,)), ((), ())),
                           preferred_element_type=F32)


def _dot_tn(a, b):
    return lax.dot_general(a.astype(BF16), b.astype(BF16), (((0,), (0,)), ((), ())),
                           preferred_element_type=F32)


def _dot_exact(a, b):
    return jnp.dot(a, b, precision=lax.Precision.HIGHEST, preferred_element_type=F32)


_DIMS = {"nn": (((1,), (0,)), ((), ())), "nt": (((1,), (1,)), ((), ())), "tn": (((0,), (0,)), ((), ()))}

def _mm(a, b, form, passes):
    dims = _DIMS[form]
    a_hi = a.astype(BF16)
    b_hi = b.astype(BF16)
    out = lax.dot_general(a_hi, b_hi, dims, preferred_element_type=F32)
    if passes == 3:
        a_lo = (a - a_hi.astype(F32)).astype(BF16)
        b_lo = (b - b_hi.astype(F32)).astype(BF16)
        out = out + (lax.dot_general(a_hi, b_lo, dims, preferred_element_type=F32)
                     + lax.dot_general(a_lo, b_hi, dims, preferred_element_type=F32))
    return out


def _silu(x):
    return x * jax.nn.sigmoid(x)


def _rel_bucket_np(dist):
    n = np.maximum(dist, 0)
    max_exact = REL_BUCKETS // 2
    nf = np.maximum(n, 1).astype(np.float32)
    large = max_exact + (np.log(nf / np.float32(max_exact)) / np.float32(math.log(REL_MAX_DIST / max_exact))
                         * np.float32(REL_BUCKETS - max_exact)).astype(np.int32)
    return np.where(n < max_exact, n, np.minimum(large, REL_BUCKETS - 1)).astype(np.int32)


def _ada_kernel(c_ref, w_ref, b_ref, o_ref):
    o_ref[...] = _dot(_silu(c_ref[...]), w_ref[...]) + b_ref[...]


def ada_mod(c_all, w_ada, b_ada):
    rows, d = c_all.shape
    n = w_ada.shape[1]
    tn = 1024
    return pl.pallas_call(
        _ada_kernel,
        out_shape=jax.ShapeDtypeStruct((rows, n), F32),
        grid=(n // tn,),
        in_specs=[pl.BlockSpec((rows, d), lambda j: (0, 0)),
                  pl.BlockSpec((d, tn), lambda j: (0, j)),
                  pl.BlockSpec((1, tn), lambda j: (0, j))],
        out_specs=pl.BlockSpec((rows, tn), lambda j: (0, j)),
        compiler_params=_cparams(("arbitrary",)),
        name="ada_mod",
    )(c_all, w_ada, b_ada.reshape(1, n))


IN_TILE = 1024
IN_GROUPS = (("a_q", 0, 1), ("a_k", 1, 1), ("a_v", 2, 1), ("g_qkv", 3, 3), ("g_z", 6, 1),
             ("gate_a", 7, 2), ("gate_b", 9, 2))
IN_COL_TILES = 11


def _inproj_kernel(x_ref, sc_ref, sh_ref, nw_ref, w_ref, ws_ref, *rest):
    out_refs = rest[:len(IN_GROUPS)]
    small_ref = rest[len(IN_GROUPS)]
    h_ref = rest[len(IN_GROUPS) + 1]
    bt, tl, d = x_ref.shape
    j = pl.program_id(2)

    @pl.when(j == 0)
    def _():
        x = x_ref[...]
        y = x * lax.rsqrt(jnp.mean(x * x, axis=-1, keepdims=True) + EPS) * nw_ref[...]
        h = y * (1.0 + sc_ref[:, 0]) + sh_ref[:, 0]
        h2 = h.reshape(bt * tl, d).astype(BF16)
        h_ref[...] = h2
        small_ref[...] = jnp.dot(h2, ws_ref[...], preferred_element_type=F32).reshape(bt, tl, LANES)

    res = jnp.dot(h_ref[...], w_ref[...], preferred_element_type=F32).reshape(bt, tl, IN_TILE)
    for o_ref, (_, j0, nj) in zip(out_refs, IN_GROUPS):
        @pl.when((j >= j0) & (j < j0 + nj))
        def _(o_ref=o_ref):
            o_ref[...] = res


def in_proj(x, mod4, norm_w, w_main, w_small, bt, tl):
    b, l, d = x.shape
    grid = (b // bt, l // tl, IN_COL_TILES)
    out_shapes, out_specs = [], []
    for _, j0, nj in IN_GROUPS:
        out_shapes.append(jax.ShapeDtypeStruct((b, l, nj * IN_TILE), F32))
        out_specs.append(pl.BlockSpec((bt, tl, IN_TILE),
                                      lambda i, t, j, j0=j0, nj=nj: (i, t, jnp.clip(j - j0, 0, nj - 1))))
    out_shapes.append(jax.ShapeDtypeStruct((b, l, LANES), F32))
    out_specs.append(pl.BlockSpec((bt, tl, LANES), lambda i, t, j: (i, t, 0)))
    return pl.pallas_call(
        _inproj_kernel,
        out_shape=out_shapes,
        grid=grid,
        in_specs=[pl.BlockSpec((bt, tl, d), lambda i, t, j: (i, t, 0)),
                  pl.BlockSpec((bt, 1, 1, d), lambda i, t, j: (i, 1, 0, 0)),
                  pl.BlockSpec((bt, 1, 1, d), lambda i, t, j: (i, 0, 0, 0)),
                  pl.BlockSpec((1, d), lambda i, t, j: (0, 0)),
                  pl.BlockSpec((d, IN_TILE), lambda i, t, j: (0, j)),
                  pl.BlockSpec((d, LANES), lambda i, t, j: (0, 0))],
        out_specs=out_specs,
        scratch_shapes=[pltpu.VMEM((bt * tl, d), BF16)],
        compiler_params=_cparams(("arbitrary", "arbitrary", "arbitrary")),
        name="in_proj",
    )(x, mod4, mod4, norm_w.reshape(1, d), w_main, w_small)


def _softmax_step(s, v_blk, m, l, acc):
    m_new = jnp.maximum(m, jnp.max(s, axis=-1, keepdims=True))
    alpha = jnp.exp(m - m_new)
    p = jnp.exp(s - m_new)
    l = alpha * l + jnp.sum(p, axis=-1, keepdims=True)
    acc = alpha * acc + _dot(p, v_blk)
    return m_new, l, acc


def _topk_select(cols, k):
    sels = []
    for n, gn in enumerate(cols):
        rank = jnp.zeros(gn.shape, F32)
        for m_, gm in enumerate(cols):
            if m_ == n:
                continue
            ahead = (gm >= gn) if m_ < n else (gm > gn)
            rank = rank + jnp.where(ahead, 1.0, 0.0)
        sels.append(rank < float(k))
    return sels


def _moba_prompt_kernel(rb_ref, bkt_ref, q_ref, k_ref, v_ref, o_ref, bias_ref):
    h = pl.program_id(1)
    s_len = q_ref.shape[1]
    blk = MOBA_BLOCK
    nb = s_len // blk
    scale = HEAD_DIM ** -0.5

    for t in range(2):
        bkt = bkt_ref[t]
        bias = jnp.zeros((blk, blk), F32)
        for r in range(REL_BUCKETS):
            bias = jnp.where(bkt == r, rb_ref[r, h], bias)
        bias_ref[t] = bias
    bias_far = rb_ref[REL_BUCKETS - 1, h]

    row = lax.broadcasted_iota(I32, (blk, blk), 0)
    col = lax.broadcasted_iota(I32, (blk, blk), 1)
    causal = col <= row

    k_means = [jnp.mean(k_ref[0, n * blk:(n + 1) * blk, :], axis=0, keepdims=True) for n in range(nb)]

    for qb in range(nb):
        q = q_ref[0, qb * blk:(qb + 1) * blk, :]
        qs = (q * scale).astype(BF16)
        if qb > MOBA_TOPK:
            gates = [jnp.sum(q * k_means[n], axis=-1, keepdims=True) for n in range(qb)]
            sels = _topk_select(gates, MOBA_TOPK)
        else:
            sels = [None] * qb
        s = _dot_nt(qs, k_ref[0, qb * blk:(qb + 1) * blk, :]) + bias_ref[0]
        s = jnp.where(causal, s, NEG_INF)
        m = jnp.max(s, axis=-1, keepdims=True)
        p = jnp.exp(s - m)
        l = jnp.sum(p, axis=-1, keepdims=True)
        acc = _dot(p, v_ref[0, qb * blk:(qb + 1) * blk, :])
        for n in range(qb - 1, -1, -1):
            s = _dot_nt(qs, k_ref[0, n * blk:(n + 1) * blk, :])
            s = s + (bias_ref[1] if n == qb - 1 else bias_far)
            if sels[n] is not None:
                s = jnp.where(sels[n], s, NEG_INF)
            m, l, acc = _softmax_step(s, v_ref[0, n * blk:(n + 1) * blk, :], m, l, acc)
        o_ref[0, qb * blk:(qb + 1) * blk, :] = acc / l


def moba_prompt(q, k, v, rel_bias):
    b, s_len, width = q.shape
    assert s_len % MOBA_BLOCK == 0 and width == A_HEADS * HEAD_DIM
    ar = np.arange(MOBA_BLOCK)
    d_loc = ar[:, None] - ar[None, :]
    bkt = np.stack([_rel_bucket_np(d_loc), _rel_bucket_np(d_loc + MOBA_BLOCK)]).astype(np.int32)
    assert int(_rel_bucket_np(np.array([MOBA_BLOCK + 1]))[0]) == REL_BUCKETS - 1
    spec = pl.BlockSpec((1, s_len, HEAD_DIM), lambda i, h: (i, 0, h))
    return pl.pallas_call(
        _moba_prompt_kernel,
        out_shape=jax.ShapeDtypeStruct((b, s_len, width), F32),
        grid=(b, A_HEADS),
        in_specs=[pl.BlockSpec(memory_space=pltpu.SMEM),
                  pl.BlockSpec((2, MOBA_BLOCK, MOBA_BLOCK), lambda i, h: (0, 0, 0)),
                  spec, spec, spec],
        out_specs=spec,
        scratch_shapes=[pltpu.VMEM((2, MOBA_BLOCK, MOBA_BLOCK), F32)],
        compiler_params=_cparams(("arbitrary", "arbitrary")),
        name="moba_prompt",
    )(rel_bias, jnp.asarray(bkt), q, k, v)


def _diag_heads(full, t):
    return jnp.concatenate([full[h * t:(h + 1) * t, h * HEAD_DIM:(h + 1) * HEAD_DIM] for h in range(A_HEADS)],
                           axis=0)


def _moba_sample_kernel(pt_ref, rb_ref, bkt_ref, q_ref, kn_ref, vn_ref, k0_ref, k1_ref, v0_ref, v1_ref, o_ref,
                        qf_ref, qb_ref, bias_ref, m_ref, l_ref, g_ref, acc_ref, *, past_len):
    n = pl.program_id(1)
    nb = pl.num_programs(1)
    t = q_ref.shape[1]
    rows = A_HEADS * t
    width = A_HEADS * HEAD_DIM
    blk = MOBA_BLOCK
    scale = HEAD_DIM ** -0.5
    n_last = past_len // blk - 1

    @pl.when(n == 0)
    def _():
        q8 = jnp.concatenate([q_ref[0]] * A_HEADS, axis=0)
        r_head = lax.broadcasted_iota(I32, (rows, width), 0) // t
        c_head = lax.broadcasted_iota(I32, (rows, width), 1) // HEAD_DIM
        qf = jnp.where(r_head == c_head, q8, 0.0)
        qf_ref[...] = qf
        qb_ref[...] = (qf * scale).astype(BF16)
        for h in range(A_HEADS):
            bkt = bkt_ref[...]
            bias = jnp.zeros((t, blk), F32)
            for r in range(REL_BUCKETS):
                bias = jnp.where(bkt == r, rb_ref[r, h], bias)
            bias_ref[h * t:(h + 1) * t, :] = bias

    k_blk = jnp.concatenate([k0_ref[0], k1_ref[0]], axis=0)
    v_blk = jnp.concatenate([v0_ref[0], v1_ref[0]], axis=0)
    k_mean = jnp.mean(k_blk, axis=0, keepdims=True)
    g_ref[n] = jnp.sum(qf_ref[...] * k_mean, axis=-1, keepdims=True)
    far_col = jnp.concatenate([jnp.full((t, 1), rb_ref[REL_BUCKETS - 1, h], F32) for h in range(A_HEADS)], axis=0)
    s = _dot_nt(qb_ref[...], k_blk)
    s = s + jnp.where(n == n_last, bias_ref[...], far_col)
    m = jnp.max(s, axis=-1, keepdims=True)
    p = jnp.exp(s - m)
    m_ref[n] = m
    l_ref[n] = jnp.sum(p, axis=-1, keepdims=True)
    acc_ref[n] = _diag_heads(_dot(p, v_blk), t)

    @pl.when(n == nb - 1)
    def _():
        n_blocks = past_len // blk
        gates = [g_ref[i] for i in range(n_blocks)]
        sels = _topk_select(gates, MOBA_TOPK)
        s_loc = _dot_nt(qb_ref[...], kn_ref[0])
        tq = lax.broadcasted_iota(I32, (rows, t), 0) % t
        tk = lax.broadcasted_iota(I32, (rows, t), 1)
        r_head = lax.broadcasted_iota(I32, (rows, t), 0) // t
        bias_loc = jnp.zeros((rows, t), F32)
        for h in range(A_HEADS):
            for d in range(t):
                bias_loc = jnp.where((r_head == h) & (tq - tk == d), rb_ref[d, h], bias_loc)
        s_loc = jnp.where(tk <= tq, s_loc + bias_loc, NEG_INF)
        m_tot = jnp.max(s_loc, axis=-1, keepdims=True)
        for i in range(n_blocks):
            m_tot = jnp.maximum(m_tot, jnp.where(sels[i], m_ref[i], NEG_INF))
        p_loc = jnp.exp(s_loc - m_tot)
        l_tot = jnp.sum(p_loc, axis=-1, keepdims=True)
        acc = _diag_heads(_dot(p_loc, vn_ref[0]), t)
        for i in range(n_blocks):
            w = jnp.where(sels[i], jnp.exp(jnp.where(sels[i], m_ref[i] - m_tot, 0.0)), 0.0)
            l_tot = l_tot + w * l_ref[i]
            acc = acc + w * acc_ref[i]
        out = acc / l_tot
        o_ref[0] = jnp.concatenate([out[h * t:(h + 1) * t, :] for h in range(A_HEADS)], axis=1)


def moba_sample(q, k_new, v_new, cache_k, cache_v, page_table, rel_bias, past_len):
    db, t, width = q.shape
    page = cache_k.shape[1]
    assert MOBA_BLOCK == 2 * page and past_len % MOBA_BLOCK == 0 and t <= REL_BUCKETS // 2
    assert past_len // MOBA_BLOCK >= MOBA_TOPK and t % SUBLANES == 0
    n_blocks = past_len // MOBA_BLOCK
    rows = A_HEADS * t
    d_last = MOBA_BLOCK + np.arange(t)[:, None] - np.arange(MOBA_BLOCK)[None, :]
    bkt = _rel_bucket_np(d_last).astype(np.int32)
    assert int(_rel_bucket_np(np.array([MOBA_BLOCK + 1]))[0]) == REL_BUCKETS - 1
    tok_spec = pl.BlockSpec((1, t, width), lambda i, n, pt: (i, 0, 0))

    def page_spec(j):
        return pl.BlockSpec((1, page, width), lambda i, n, pt: (pt[i, 2 * n + j], 0, 0))

    grid_spec = pltpu.PrefetchScalarGridSpec(
        num_scalar_prefetch=1,
        grid=(db, n_blocks),
        in_specs=[pl.BlockSpec(memory_space=pltpu.SMEM),
                  pl.BlockSpec((t, MOBA_BLOCK), lambda i, n, pt: (0, 0)),
                  tok_spec, tok_spec, tok_spec,
                  page_spec(0), page_spec(1), page_spec(0), page_spec(1)],
        out_specs=tok_spec,
        scratch_shapes=[pltpu.VMEM((rows, width), F32),
                        pltpu.VMEM((rows, width), BF16),
                        pltpu.VMEM((rows, MOBA_BLOCK), F32),
                        pltpu.VMEM((n_blocks, rows, 1), F32),
                        pltpu.VMEM((n_blocks, rows, 1), F32),
                        pltpu.VMEM((n_blocks, rows, 1), F32),
                        pltpu.VMEM((n_blocks, rows, HEAD_DIM), F32)],
    )
    return pl.pallas_call(
        functools.partial(_moba_sample_kernel, past_len=past_len),
        out_shape=jax.ShapeDtypeStruct((db, t, width), F32),
        grid_spec=grid_spec,
        compiler_params=_cparams(("arbitrary", "arbitrary")),
        name="moba_sample",
    )(page_table, rel_bias, jnp.asarray(bkt), q, k_new, v_new, cache_k, cache_k, cache_v, cache_v)


INV_BASE = 16


def _unit_lower_inverse(a_mat, ri, ci, size):
    base = min(INV_BASE, size)
    a_d = jnp.where(ri // base == ci // base, a_mat, 0.0)
    inv = jnp.where(ri == ci, 1.0, 0.0) - a_d
    pw = a_d
    for _ in range(max(int(math.log2(base)) - 1, 0)):
        pw = _mm(pw, pw, "nn", 3)
        inv = inv + _mm(inv, pw, "nn", 3)
    blk = base
    while blk < size:
        pr = ri // blk
        pc = ci // blk
        a_off = jnp.where((pr == pc + 1) & (pr // 2 == pc // 2), a_mat, 0.0)
        inv = inv - _mm(_mm(inv, a_off, "nn", 3), inv, "nn", 3)
        blk *= 2
    return inv


def _delta_kernel(alog_ref, dtb_ref, xq_ref, xk_ref, xv_ref, cq_ref, ck_ref, cv_ref, z_ref, sm_ref, smt_ref,
                  nw_ref, s0_ref, b0q_ref, b0k_ref, b0v_ref,
                  y_ref, sn_ref, tq_ref, tk_ref, tv_ref,
                  s_sc, carry_sc):
    h = pl.program_id(1)
    c = pl.program_id(2)
    nc = pl.num_programs(2)
    chunk = xq_ref.shape[1]
    dk = HEAD_DIM

    @pl.when(c == 0)
    def _():
        s_sc[...] = s0_ref[0, 0]
        carry_sc[0] = b0q_ref[0]
        carry_sc[1] = b0k_ref[0]
        carry_sc[2] = b0v_ref[0]

    def conv(i, x_ref, cw_ref):
        x = x_ref[0]
        xx = jnp.concatenate([carry_sc[i], x], axis=0)
        cw = cw_ref[...]
        y = x * cw[CONV_W - 1:CONV_W, :]
        for s in range(1, CONV_W):
            y = y + xx[SUBLANES - s:SUBLANES - s + chunk, :] * cw[CONV_W - 1 - s:CONV_W - s, :]
        carry_sc[i] = xx[chunk:chunk + SUBLANES, :]
        return _silu(y)

    q = conv(0, xq_ref, cq_ref)
    k = conv(1, xk_ref, ck_ref)
    v = conv(2, xv_ref, cv_ref)
    q = q * lax.rsqrt(jnp.sum(q * q, axis=-1, keepdims=True) + EPS) * (dk ** -0.5)
    k = k * lax.rsqrt(jnp.sum(k * k, axis=-1, keepdims=True) + EPS)

    def softplus(x):
        return jnp.maximum(x, 0.0) + jnp.log(1.0 + jnp.exp(-jnp.abs(x)))

    decay_rate = -jnp.exp(alog_ref[h])
    lane = lax.broadcasted_iota(I32, (chunk, LANES), 1)
    sm = sm_ref[0]
    beta_c = jnp.sum(jnp.where(lane == h, jax.nn.sigmoid(sm), 0.0), axis=-1, keepdims=True)
    g_c = jnp.sum(jnp.where(lane == G_HEADS + h, decay_rate * softplus(sm + dtb_ref[h]), 0.0),
                  axis=-1, keepdims=True)
    g_r = decay_rate * softplus(smt_ref[0, pl.ds(G_HEADS + h, 1), :] + dtb_ref[h])

    ri = lax.broadcasted_iota(I32, (chunk, chunk), 0)
    ci = lax.broadcasted_iota(I32, (chunk, chunk), 1)
    incl = ci <= ri
    strict = ci < ri
    gc_c = _dot_exact(jnp.where(incl, 1.0, 0.0), jnp.broadcast_to(g_c, (chunk, LANES)))[:, 0:1]
    gc_r = _dot_exact(jnp.broadcast_to(g_r, (SUBLANES, chunk)), jnp.where(ri <= ci, 1.0, 0.0))[0:1, :]
    dmask = jnp.exp(jnp.where(incl, gc_c - gc_r, NEG_INF))

    kb = k * beta_c
    a_mat = jnp.where(strict, _mm(kb, k, "nt", 1) * dmask, 0.0)
    t_inv = _unit_lower_inverse(a_mat, ri, ci, chunk)
    eg = jnp.exp(gc_c)
    uw = _mm(t_inv, jnp.concatenate([v * beta_c, kb * eg], axis=1), "nn", 1)
    u = uw[:, :dk]
    w = uw[:, dk:]
    s_mat = s_sc[...]
    v_new = u - _mm(w, s_mat, "nn", 1)
    attn = _mm(q, k, "nt", 1) * dmask
    o = _mm(q * eg, s_mat, "nn", 1) + _mm(attn, v_new, "nn", 1)
    g_last = gc_c[chunk - 1:chunk, :]
    s_sc[...] = s_mat * jnp.exp(g_last) + _mm(k * jnp.exp(g_last - gc_c), v_new, "tn", 1)

    o_n = o * lax.rsqrt(jnp.mean(o * o, axis=-1, keepdims=True) + EPS) * nw_ref[...]
    y_ref[0] = o_n * _silu(z_ref[0])

    @pl.when(c == nc - 1)
    def _():
        sn_ref[0, 0] = s_sc[...]
        tq_ref[0] = carry_sc[0]
        tk_ref[0] = carry_sc[1]
        tv_ref[0] = carry_sc[2]


def delta_rule(xg, z, small, small_t, conv_w, a_log, dt_bias, delta_norm, s0, buf0, chunk):
    b, l, cw_ch = xg.shape
    hh = G_HEADS
    assert l % chunk == 0 and cw_ch == 3 * hh * HEAD_DIM and l >= CONV_W - 1
    nc = l // chunk

    def xspec(part):
        return pl.BlockSpec((1, chunk, HEAD_DIM), lambda i, h, c: (i, c, part * hh + h))

    def cspec(part):
        return pl.BlockSpec((CONV_W, HEAD_DIM), lambda i, h, c: (0, part * hh + h))

    def bspec(part):
        return pl.BlockSpec((1, SUBLANES, HEAD_DIM), lambda i, h, c: (i, 0, part * hh + h))

    smem = pl.BlockSpec(memory_space=pltpu.SMEM)
    tail_shape = jax.ShapeDtypeStruct((b, SUBLANES, hh * HEAD_DIM), F32)
    tail_spec = pl.BlockSpec((1, SUBLANES, HEAD_DIM), lambda i, h, c: (i, 0, h))
    outs = pl.pallas_call(
        _delta_kernel,
        out_shape=[jax.ShapeDtypeStruct((b, l, hh * HEAD_DIM), F32),
                   jax.ShapeDtypeStruct((b, hh, HEAD_DIM, HEAD_DIM), F32),
                   tail_shape, tail_shape, tail_shape],
        grid=(b, hh, nc),
        in_specs=[smem, smem, xspec(0), xspec(1), xspec(2), cspec(0), cspec(1), cspec(2),
                  pl.BlockSpec((1, chunk, HEAD_DIM), lambda i, h, c: (i, c, h)),
                  pl.BlockSpec((1, chunk, LANES), lambda i, h, c: (i, c, 0)),
                  pl.BlockSpec((1, 2 * hh, chunk), lambda i, h, c: (i, 0, c)),
                  pl.BlockSpec((1, HEAD_DIM), lambda i, h, c: (0, 0)),
                  pl.BlockSpec((1, 1, HEAD_DIM, HEAD_DIM), lambda i, h, c: (i, h, 0, 0)),
                  bspec(0), bspec(1), bspec(2)],
        out_specs=[pl.BlockSpec((1, chunk, HEAD_DIM), lambda i, h, c: (i, c, h)),
                   pl.BlockSpec((1, 1, HEAD_DIM, HEAD_DIM), lambda i, h, c: (i, h, 0, 0)),
                   tail_spec, tail_spec, tail_spec],
        scratch_shapes=[pltpu.VMEM((HEAD_DIM, HEAD_DIM), F32),
                        pltpu.VMEM((3, SUBLANES, HEAD_DIM), F32)],
        compiler_params=_cparams(("arbitrary", "arbitrary", "arbitrary")),
        name="delta_rule",
    )(a_log, dt_bias, xg, xg, xg, conv_w, conv_w, conv_w, z, small, small_t,
      delta_norm.reshape(1, HEAD_DIM), s0, buf0, buf0, buf0)
    y_b, s_new, tq, tk, tv = outs
    return y_b, s_new, jnp.concatenate([tq, tk, tv], axis=-1)


def _post_kernel(ya_ref, yb_ref, ga_ref, gb_ref, x_ref, gm_ref, scf_ref, shf_ref, wa_ref, wb_ref, wo_ref,
                 nw_ref, rw_ref, x1_ref, h2_ref, lg_ref):
    bt, tl, d = x_ref.shape
    rows = bt * tl
    ya = ya_ref[...].reshape(rows, -1)
    yb = yb_ref[...].reshape(rows, -1)
    merged = (jax.nn.sigmoid(ga_ref[...].reshape(rows, d)) * _dot(ya, wa_ref[...])
              + jax.nn.sigmoid(gb_ref[...].reshape(rows, d)) * _dot(yb, wb_ref[...]))
    mix = _dot(merged, wo_ref[...]).reshape(bt, tl, d)
    x1 = x_ref[...] + gm_ref[:, 0] * mix
    x1_ref[...] = x1
    y = x1 * lax.rsqrt(jnp.mean(x1 * x1, axis=-1, keepdims=True) + EPS) * nw_ref[...]
    h2 = y * (1.0 + scf_ref[:, 0]) + shf_ref[:, 0]
    h2_ref[...] = h2
    lg_ref[...] = _dot_exact(h2.reshape(rows, d), rw_ref[...]).reshape(bt, tl, -1)


def post_mix(y_a, y_b, gate_a, gate_b, x, mod4, w_a, w_b, w_o, norm_w, router_w, bt, tl):
    b, l, d = x.shape
    wa_in = y_a.shape[-1]
    wb_in = y_b.shape[-1]
    n_e = router_w.shape[1]

    def tok(width):
        return pl.BlockSpec((bt, tl, width), lambda i, t: (i, t, 0))

    def modspec(idx):
        return pl.BlockSpec((bt, 1, 1, d), lambda i, t, idx=idx: (i, idx, 0, 0))

    def const(shape):
        return pl.BlockSpec(shape, lambda i, t: (0, 0), pipeline_mode=pl.Buffered(1))

    return pl.pallas_call(
        _post_kernel,
        out_shape=[jax.ShapeDtypeStruct((b, l, d), F32), jax.ShapeDtypeStruct((b, l, d), F32),
                   jax.ShapeDtypeStruct((b, l, n_e), F32)],
        grid=(b // bt, l // tl),
        in_specs=[tok(wa_in), tok(wb_in), tok(d), tok(d), tok(d), modspec(2), modspec(4), modspec(3),
                  const((wa_in, d)), const((wb_in, d)), const((d, d)), const((1, d)), const((d, n_e))],
        out_specs=[tok(d), tok(d), tok(n_e)],
        compiler_params=_cparams(("arbitrary", "arbitrary")),
        name="post_mix",
    )(y_a, y_b, gate_a, gate_b, x, mod4, mod4, mod4, w_a, w_b, w_o, norm_w.reshape(1, d), router_w)


def _route_kernel(lg_ref, rb_ref, e_ref, w_ref, p_ref, cnt_ref, run_sc, *, n_e):
    i = pl.program_id(0)
    tm, width = lg_ref.shape
    per_group = n_e // N_GROUPS

    @pl.when(i == 0)
    def _():
        run_sc[...] = jnp.zeros_like(run_sc)

    lane = lax.broadcasted_iota(I32, (tm, width), 1)
    scores = jnp.where(lane < n_e, jax.nn.sigmoid(lg_ref[...]), 0.0)
    biased = jnp.where(lane < n_e, scores + rb_ref[...], NEG_INF)
    grp = lane // per_group

    def first_argmax(vals):
        mx = jnp.max(vals, axis=-1, keepdims=True)
        idx = jnp.min(jnp.where(vals == mx, lane, width), axis=-1, keepdims=True)
        return mx, idx

    group_scores = []
    for g in range(N_GROUPS):
        mg = jnp.where(grp == g, biased, NEG_INF)
        m1, i1 = first_argmax(mg)
        m2 = jnp.max(jnp.where(lane == i1, NEG_INF, mg), axis=-1, keepdims=True)
        group_scores.append(m1 + m2)
    keep = _topk_select(group_scores, TOPK_GROUPS)
    expert_keep = jnp.zeros((tm, width), jnp.bool_)
    for g in range(N_GROUPS):
        expert_keep = expert_keep | ((grp == g) & keep[g])
    masked = jnp.where(expert_keep, biased, NEG_INF)

    sel = jnp.zeros((tm, width), jnp.bool_)
    idxs = []
    for _ in range(TOP_K):
        _, idx = first_argmax(masked)
        hit = lane == idx
        sel = sel | hit
        masked = jnp.where(hit, NEG_INF, masked)
        idxs.append(idx)
    sel_f = jnp.where(sel, 1.0, 0.0)
    top_w = scores * sel_f
    top_w = top_w / jnp.sum(top_w, axis=-1, keepdims=True) * ROUTED_SCALE

    ri = lax.broadcasted_iota(I32, (tm, tm), 0)
    ci = lax.broadcasted_iota(I32, (tm, tm), 1)
    before = _dot(jnp.where(ci < ri, 1.0, 0.0), sel_f) + run_sc[...]
    run_sc[...] = run_sc[...] + jnp.sum(sel_f, axis=0, keepdims=True)
    cnt_ref[...] = run_sc[...].astype(I32)

    lane_o = lax.broadcasted_iota(I32, (tm, LANES), 1)
    e_out = jnp.zeros((tm, LANES), I32)
    w_out = jnp.zeros((tm, LANES), F32)
    p_out = jnp.zeros((tm, LANES), I32)
    for kk, idx in enumerate(idxs):
        hit = lane == idx
        wk = jnp.sum(jnp.where(hit, top_w, 0.0), axis=-1, keepdims=True)
        pk = jnp.sum(jnp.where(hit, before, 0.0), axis=-1, keepdims=True).astype(I32)
        e_out = jnp.where(lane_o == kk, idx, e_out)
        w_out = jnp.where(lane_o == kk, wk, w_out)
        p_out = jnp.where(lane_o == kk, pk, p_out)
    e_ref[...] = e_out
    w_ref[...] = w_out
    p_ref[...] = p_out


def route(logits, router_bias, n_e, tm):
    n, width = logits.shape
    assert n % tm == 0 and width == LANES and n_e <= LANES
    tok = pl.BlockSpec((tm, LANES), lambda i: (i, 0))
    one = pl.BlockSpec((1, LANES), lambda i: (0, 0))
    bias = jnp.pad(router_bias.reshape(1, n_e), ((0, 0), (0, LANES - n_e)))
    return pl.pallas_call(
        functools.partial(_route_kernel, n_e=n_e),
        out_shape=[jax.ShapeDtypeStruct((n, LANES), I32), jax.ShapeDtypeStruct((n, LANES), F32),
                   jax.ShapeDtypeStruct((n, LANES), I32), jax.ShapeDtypeStruct((1, LANES), I32)],
        grid=(n // tm,),
        in_specs=[tok, one],
        out_specs=[tok, tok, tok, one],
        scratch_shapes=[pltpu.VMEM((1, LANES), F32)],
        compiler_params=_cparams(("arbitrary",)),
        name="route",
    )(logits, bias)


def _row_copy(src_ref, dst_ref, src_row, dst_row, sem):
    return pltpu.make_async_copy(src_ref.at[pl.ds(src_row, 1)], dst_ref.at[pl.ds(dst_row, 1)], sem)


def _dispatch_kernel(dest_ref, h_ref, hs_in_ref, hs_ref, sem):
    del hs_in_ref
    tm = h_ref.shape[0]

    def body(t, carry):
        for kk in range(TOP_K):
            _row_copy(h_ref, hs_ref, t, dest_ref[0, 0, t * TOP_K + kk], sem).start()
        return carry

    def drain(t, carry):
        for kk in range(TOP_K):
            _row_copy(h_ref, hs_ref, t, dest_ref[0, 0, t * TOP_K + kk], sem).wait()
        return carry

    lax.fori_loop(0, tm, body, 0)
    lax.fori_loop(0, tm, drain, 0)


def dispatch(h_all, dest, n_rows, tm):
    n, d = h_all.shape
    assert n % tm == 0
    dest3 = dest.reshape(n // tm, 1, tm * TOP_K)
    zeros = jnp.zeros((n_rows, d), F32)
    return pl.pallas_call(
        _dispatch_kernel,
        out_shape=jax.ShapeDtypeStruct((n_rows, d), F32),
        grid=(n // tm,),
        in_specs=[pl.BlockSpec((1, 1, tm * TOP_K), lambda i: (i, 0, 0), memory_space=pltpu.SMEM),
                  pl.BlockSpec((tm, d), lambda i: (i, 0)),
                  pl.BlockSpec(memory_space=pl.ANY)],
        out_specs=pl.BlockSpec(memory_space=pl.ANY),
        scratch_shapes=[pltpu.SemaphoreType.DMA(())],
        input_output_aliases={2: 0},
        compiler_params=_cparams(("arbitrary",)),
        name="dispatch",
    )(dest3, h_all, zeros)


def _experts_kernel(be_ref, bf_ref, nu_ref, x_ref, wg_ref, wu_ref, wd_ref, y_ref, wg_sc, wu_sc, wd_sc):
    i = pl.program_id(0)

    @pl.when(bf_ref[i] == 1)
    def _():
        wg_sc[...] = wg_ref[0].astype(BF16)
        wu_sc[...] = wu_ref[0].astype(BF16)
        wd_sc[...] = wd_ref[0].astype(BF16)

    @pl.when(i < nu_ref[0])
    def _():
        x = x_ref[...].astype(BF16)
        g = jnp.dot(x, wg_sc[...], preferred_element_type=F32)
        u = jnp.dot(x, wu_sc[...], preferred_element_type=F32)
        y_ref[...] = jnp.dot((_silu(g) * u).astype(BF16), wd_sc[...], preferred_element_type=F32)

    @pl.when(i >= nu_ref[0])
    def _():
        y_ref[...] = jnp.zeros_like(y_ref)


def experts(h_sorted, blk_e, blk_first, n_used, w_gate, w_up, w_down):
    n_rows, d = h_sorted.shape
    n_blocks = n_rows // EXPERT_ROWS
    de = w_gate.shape[-1]
    grid_spec = pltpu.PrefetchScalarGridSpec(
        num_scalar_prefetch=3,
        grid=(n_blocks,),
        in_specs=[pl.BlockSpec((EXPERT_ROWS, d), lambda i, be, bf, nu: (i, 0)),
                  pl.BlockSpec((1, d, de), lambda i, be, bf, nu: (be[i], 0, 0)),
                  pl.BlockSpec((1, d, de), lambda i, be, bf, nu: (be[i], 0, 0)),
                  pl.BlockSpec((1, de, d), lambda i, be, bf, nu: (be[i], 0, 0))],
        out_specs=pl.BlockSpec((EXPERT_ROWS, d), lambda i, be, bf, nu: (i, 0)),
        scratch_shapes=[pltpu.VMEM((d, de), BF16), pltpu.VMEM((d, de), BF16), pltpu.VMEM((de, d), BF16)],
    )
    return pl.pallas_call(
        _experts_kernel,
        out_shape=jax.ShapeDtypeStruct((n_rows, d), F32),
        grid_spec=grid_spec,
        compiler_params=_cparams(("arbitrary",)),
        name="experts",
    )(blk_e, blk_first, n_used, h_sorted, w_gate, w_up, w_down)


def _combine_kernel(dest_ref, x1_ref, h2_ref, gf_ref, w_ref, wg_ref, wu_ref, wd_ref, nw_ref, yr_ref, o_ref,
                    gbuf, sem):
    bt, tl, d = x1_ref.shape
    tm = bt * tl

    def body(t, carry):
        for kk in range(TOP_K):
            _row_copy(yr_ref, gbuf.at[kk], dest_ref[0, 0, t * TOP_K + kk], t, sem).start()
        return carry

    lax.fori_loop(0, tm, body, 0)
    h2 = h2_ref[...].reshape(tm, d).astype(BF16)
    g = jnp.dot(h2, wg_ref[...], preferred_element_type=F32)
    u = jnp.dot(h2, wu_ref[...], preferred_element_type=F32)
    y = jnp.dot((_silu(g) * u).astype(BF16), wd_ref[...], preferred_element_type=F32)
    wts = w_ref[...]

    def drain(t, carry):
        for kk in range(TOP_K):
            _row_copy(yr_ref, gbuf.at[kk], dest_ref[0, 0, t * TOP_K + kk], t, sem).wait()
        return carry

    lax.fori_loop(0, tm, drain, 0)
    for kk in range(TOP_K):
        y = y + gbuf[kk] * wts[:, kk:kk + 1]
    x2 = x1_ref[...] + gf_ref[:, 0] * y.reshape(bt, tl, d)
    o_ref[...] = x2 * lax.rsqrt(jnp.mean(x2 * x2, axis=-1, keepdims=True) + EPS) * nw_ref[...]


def combine(x1, h2, mod4, sel_w, dest, y_rows, w_gs, w_us, w_ds, norm_final, bt, tl):
    b, l, d = x1.shape
    tm = bt * tl
    n = b * l
    ds = w_gs.shape[1]
    n_t = l // tl
    dest3 = dest.reshape(n // tm, 1, tm * TOP_K)

    def const(shape):
        return pl.BlockSpec(shape, lambda i, t: (0, 0), pipeline_mode=pl.Buffered(1))

    return pl.pallas_call(
        _combine_kernel,
        out_shape=jax.ShapeDtypeStruct((b, l, d), F32),
        grid=(b // bt, n_t),
        in_specs=[pl.BlockSpec((1, 1, tm * TOP_K), lambda i, t: (i * n_t + t, 0, 0), memory_space=pltpu.SMEM),
                  pl.BlockSpec((bt, tl, d), lambda i, t: (i, t, 0)),
                  pl.BlockSpec((bt, tl, d), lambda i, t: (i, t, 0)),
                  pl.BlockSpec((bt, 1, 1, d), lambda i, t: (i, 5, 0, 0)),
                  pl.BlockSpec((tm, LANES), lambda i, t: (i * n_t + t, 0)),
                  const((d, ds)), const((d, ds)), const((ds, d)), const((1, d)),
                  pl.BlockSpec(memory_space=pl.ANY)],
        out_specs=pl.BlockSpec((bt, tl, d), lambda i, t: (i, t, 0)),
        scratch_shapes=[pltpu.VMEM((TOP_K, tm, d), F32), pltpu.SemaphoreType.DMA(())],
        compiler_params=_cparams(("arbitrary", "arbitrary")),
        name="combine",
    )(dest3, x1, h2, mod4, sel_w, w_gs, w_us, w_ds, norm_final.reshape(1, d), y_rows)


def _moe(h2_p, h2_s, x1_p, x1_s, lg_p, lg_s, mod_p, mod_s, router_bias, w_gate_e, w_up_e, w_down_e,
         w_gs, w_us, w_ds, norm_final, tiles_p, tiles_s):
    d = h2_p.shape[-1]
    n_p = h2_p.shape[0] * h2_p.shape[1]
    n_s = h2_s.shape[0] * h2_s.shape[1]
    n = n_p + n_s
    n_e = w_gate_e.shape[0]
    h_all = jnp.concatenate([h2_p.reshape(n_p, d), h2_s.reshape(n_s, d)], axis=0)
    logits = jnp.concatenate([lg_p.reshape(n_p, LANES), lg_s.reshape(n_s, LANES)], axis=0)
    sel_e, sel_w, sel_pos, counts = route(logits, router_bias, n_e, 512)
    counts = counts[0, :n_e]
    padded = (counts + EXPERT_ROWS - 1) // EXPERT_ROWS * EXPERT_ROWS
    pad_ends = jnp.cumsum(padded)
    pad_starts = pad_ends - padded
    n_blocks = -(-(n * TOP_K + n_e * (EXPERT_ROWS - 1)) // EXPERT_ROWS)
    n_rows = n_blocks * EXPERT_ROWS
    dest = pad_starts[sel_e[:, :TOP_K]] + sel_pos[:, :TOP_K]
    blk_start = jnp.arange(n_blocks, dtype=I32) * EXPERT_ROWS
    n_used = (pad_ends[-1] // EXPERT_ROWS).astype(I32)
    blk_e = jnp.minimum(jnp.searchsorted(pad_ends, blk_start, side="right"), n_e - 1).astype(I32)
    last_used_e = blk_e[jnp.maximum(n_used - 1, 0)]
    blk_e = jnp.where(jnp.arange(n_blocks) < n_used, blk_e, last_used_e)
    blk_first = jnp.concatenate([jnp.ones((1,), I32), (blk_e[1:] != blk_e[:-1]).astype(I32)])
    h_sorted = dispatch(h_all, dest, n_rows, 512)
    y_rows = experts(h_sorted, blk_e, blk_first, n_used.reshape(1), w_gate_e, w_up_e, w_down_e)
    y_p = combine(x1_p, h2_p, mod_p, sel_w[:n_p], dest[:n_p], y_rows, w_gs, w_us, w_ds, norm_final, *tiles_p)
    y_s = combine(x1_s, h2_s, mod_s, sel_w[n_p:], dest[n_p:], y_rows, w_gs, w_us, w_ds, norm_final, *tiles_s)
    return y_p, y_s


def kernel(x_prompt, x_sample, cache_k, cache_v, state_delta, state_conv, page_table, c_prompt, c_sample,
           w_ada, b_ada, norm_mix, w_in, conv_w, a_log, dt_bias, delta_norm, w_branch_a, w_branch_b, w_out,
           rel_bias, norm_ffn, router_w, router_bias, w_gate_e, w_up_e, w_down_e, w_gate_s, w_up_s, w_down_s,
           norm_final):
    depth = w_ada.shape[0]
    assert depth == 1, "the final norm is fused into the layer's last stage"
    n_b, seq, d = x_prompt.shape
    d_b, d_seq, _ = x_sample.shape
    n_pages = page_table.shape[1]
    page = cache_k.shape[2]
    past_len = n_pages * page
    a_width = A_HEADS * HEAD_DIM
    g_width = G_HEADS * HEAD_DIM
    conv_ch = 3 * g_width
    l = 0

    off_small = 3 * a_width + conv_ch + g_width
    w_l = w_in[l]
    w_main = jnp.concatenate([w_l[:, :off_small], w_l[:, off_small + 2 * G_HEADS:]], axis=1).astype(BF16)
    w_small = jnp.pad(w_l[:, off_small:off_small + 2 * G_HEADS], ((0, 0), (0, LANES - 2 * G_HEADS))).astype(BF16)
    assert w_main.shape[1] == IN_COL_TILES * IN_TILE

    c_all = jnp.concatenate([c_prompt, c_sample], axis=0)
    mod = ada_mod(c_all, w_ada[l], b_ada[l])
    mod_p = mod[:n_b].reshape(n_b, N_ADA, 1, d)
    mod_s = mod[n_b:].reshape(d_b, N_ADA, 1, d)

    tiles_p = (1, 512)
    tiles_s = (512 // d_seq, d_seq)
    wa = w_branch_a[l].astype(BF16)
    wb = w_branch_b[l].astype(BF16)
    wo = w_out[l].astype(BF16)
    rw = jnp.pad(router_w[l], ((0, 0), (0, LANES - router_w.shape[-1])))

    def mixer(x, mod4, tiles, attend, s0, buf0, chunk):
        a_q, a_k, a_v, g_qkv, g_z, gate_a, gate_b, small = in_proj(x, mod4, norm_mix[l], w_main, w_small, *tiles)
        y_a = attend(a_q, a_k, a_v)
        small_t = jnp.swapaxes(small[..., :2 * G_HEADS], 1, 2)
        y_b, s_new, tail = delta_rule(g_qkv, g_z, small, small_t, conv_w[l], a_log[l], dt_bias[l],
                                      delta_norm[l], s0, buf0, chunk)
        post_tiles = (tiles[0], tiles[1] // 2) if tiles[0] == 1 else (tiles[0] // 2, tiles[1])
        x1, h2, lg = post_mix(y_a, y_b, gate_a, gate_b, x, mod4, wa, wb, wo, norm_ffn[l], rw, *post_tiles)
        return x1, h2, lg, a_k, a_v, s_new, tail[:, SUBLANES - (CONV_W - 1):]

    s0_p = jnp.zeros((n_b, G_HEADS, HEAD_DIM, HEAD_DIM), F32)
    buf0_p = jnp.zeros((n_b, SUBLANES, conv_ch), F32)
    buf0_s = jnp.pad(state_conv[l], ((0, 0), (SUBLANES - (CONV_W - 1), 0), (0, 0)))
    ck = cache_k[l].reshape(cache_k.shape[1], page, a_width)
    cv = cache_v[l].reshape(cache_v.shape[1], page, a_width)

    x1_p, h2_p, lg_p, k_p, v_p, d_p, cv_p = mixer(
        x_prompt, mod_p, tiles_p, functools.partial(moba_prompt, rel_bias=rel_bias), s0_p, buf0_p, DELTA_CHUNK)
    x1_s, h2_s, lg_s, k_s, v_s, d_s, cv_s = mixer(
        x_sample, mod_s, tiles_s,
        functools.partial(moba_sample, cache_k=ck, cache_v=cv, page_table=page_table, rel_bias=rel_bias,
                          past_len=past_len),
        state_delta[l], buf0_s, d_seq)

    comb_p = (1, 128)
    comb_s = (128 // d_seq, d_seq)
    y_p, y_s = _moe(h2_p, h2_s, x1_p, x1_s, lg_p, lg_s, mod_p, mod_s, router_bias[l],
                    w_gate_e[l], w_up_e[l], w_down_e[l],
                    w_gate_s[l].astype(BF16), w_up_s[l].astype(BF16), w_down_s[l].astype(BF16),
                    norm_final, comb_p, comb_s)

    def heads(t, n_heads):
        return t.reshape(t.shape[:-1] + (n_heads, HEAD_DIM))[None]

    return (y_p, y_s, heads(k_p, A_HEADS), heads(v_p, A_HEADS), heads(k_s, A_HEADS), heads(v_s, A_HEADS),
            d_p[None], cv_p[None], d_s[None], cv_s[None])
```

```python
import functools
import math

import numpy as np
import jax
import jax.numpy as jnp
from jax import lax
from jax.experimental import pallas as pl
from jax.experimental.pallas import tpu as pltpu

F32 = jnp.float32
BF16 = jnp.bfloat16
I32 = jnp.int32
EPS = 1e-6
NEG_INF = float("-inf")

A_HEADS = 8
G_HEADS = 8
MOBA_BLOCK = 256
MOBA_TOPK = 3
REL_BUCKETS = 32
REL_MAX_DIST = 128
CONV_W = 4
N_GROUPS = 8
TOPK_GROUPS = 4
TOP_K = 6
ROUTED_SCALE = 2.5
N_ADA = 6
HEAD_DIM = 128

LANES = 128
SUBLANES = 8
VMEM_LIMIT = 56 * 1024 * 1024
EXPERT_ROWS = 256
DELTA_CHUNK = 128


def _cparams(sem, vmem=VMEM_LIMIT):
    return pltpu.CompilerParams(dimension_semantics=sem, vmem_limit_bytes=vmem)


def _dot(a, b):
    return jnp.dot(a.astype(BF16), b.astype(BF16), preferred_element_type=F32)


def _dot_nt(a, b):
    return lax.dot_general(a.astype(BF16), b.astype(BF16), (((1,), (1,)), ((), ())),
                           preferred_element_type=F32)


def _dot_tn(a, b):
    return lax.dot_general(a.astype(BF16), b.astype(BF16), (((0,), (0,)), ((), ())),
                           preferred_element_type=F32)


def _dot_exact(a, b):
    return jnp.dot(a, b, precision=lax.Precision.HIGHEST, preferred_element_type=F32)


_DIMS = {"nn": (((1,), (0,)), ((), ())), "nt": (((1,), (1,)), ((), ())), "tn": (((0,), (0,)), ((), ()))}

def _mm(a, b, form, passes):
    dims = _DIMS[form]
    a_hi = a.astype(BF16)
    b_hi = b.astype(BF16)
    out = lax.dot_general(a_hi, b_hi, dims, preferred_element_type=F32)
    if passes == 3:
        a_lo = (a - a_hi.astype(F32)).astype(BF16)
        b_lo = (b - b_hi.astype(F32)).astype(BF16)
        out = out + (lax.dot_general(a_hi, b_lo, dims, preferred_element_type=F32)
                     + lax.dot_general(a_lo, b_hi, dims, preferred_element_type=F32))
    return out


def _silu(x):
    return x * jax.nn.sigmoid(x)


def _rel_bucket_np(dist):
    n = np.maximum(dist, 0)
    max_exact = REL_BUCKETS // 2
    nf = np.maximum(n, 1).astype(np.float32)
    large = max_exact + (np.log(nf / np.float32(max_exact)) / np.float32(math.log(REL_MAX_DIST / max_exact))
                         * np.float32(REL_BUCKETS - max_exact)).astype(np.int32)
    return np.where(n < max_exact, n, np.minimum(large, REL_BUCKETS - 1)).astype(np.int32)


def _ada_kernel(c_ref, w_ref, b_ref, o_ref):
    o_ref[...] = _dot(_silu(c_ref[...]), w_ref[...]) + b_ref[...]


def ada_mod(c_all, w_ada, b_ada):
    rows, d = c_all.shape
    n = w_ada.shape[1]
    tn = 1024
    return pl.pallas_call(
        _ada_kernel,
        out_shape=jax.ShapeDtypeStruct((rows, n), F32),
        grid=(n // tn,),
        in_specs=[pl.BlockSpec((rows, d), lambda j: (0, 0)),
                  pl.BlockSpec((d, tn), lambda j: (0, j)),
                  pl.BlockSpec((1, tn), lambda j: (0, j))],
        out_specs=pl.BlockSpec((rows, tn), lambda j: (0, j)),
        compiler_params=_cparams(("arbitrary",)),
        name="ada_mod",
    )(c_all, w_ada, b_ada.reshape(1, n))


IN_TILE = 1024
IN_GROUPS = (("a_q", 0, 1), ("a_k", 1, 1), ("a_v", 2, 1), ("g_qkv", 3, 3), ("g_z", 6, 1),
             ("gate_a", 7, 2), ("gate_b", 9, 2))
IN_COL_TILES = 11


def _inproj_kernel(x_ref, sc_ref, sh_ref, nw_ref, w_ref, ws_ref, *rest):
    out_refs = rest[:len(IN_GROUPS)]
    small_ref = rest[len(IN_GROUPS)]
    h_ref = rest[len(IN_GROUPS) + 1]
    bt, tl, d = x_ref.shape
    j = pl.program_id(2)

    @pl.when(j == 0)
    def _():
        x = x_ref[...]
        y = x * lax.rsqrt(jnp.mean(x * x, axis=-1, keepdims=True) + EPS) * nw_ref[...]
        h = y * (1.0 + sc_ref[:, 0]) + sh_ref[:, 0]
        h2 = h.reshape(bt * tl, d).astype(BF16)
        h_ref[...] = h2
        small_ref[...] = jnp.dot(h2, ws_ref[...], preferred_element_type=F32).reshape(bt, tl, LANES)

    res = jnp.dot(h_ref[...], w_ref[...], preferred_element_type=F32).reshape(bt, tl, IN_TILE)
    for o_ref, (_, j0, nj) in zip(out_refs, IN_GROUPS):
        @pl.when((j >= j0) & (j < j0 + nj))
        def _(o_ref=o_ref):
            o_ref[...] = res


def in_proj(x, mod4, norm_w, w_main, w_small, bt, tl):
    b, l, d = x.shape
    grid = (b // bt, l // tl, IN_COL_TILES)
    out_shapes, out_specs = [], []
    for _, j0, nj in IN_GROUPS:
        out_shapes.append(jax.ShapeDtypeStruct((b, l, nj * IN_TILE), F32))
        out_specs.append(pl.BlockSpec((bt, tl, IN_TILE),
                                      lambda i, t, j, j0=j0, nj=nj: (i, t, jnp.clip(j - j0, 0, nj - 1))))
    out_shapes.append(jax.ShapeDtypeStruct((b, l, LANES), F32))
    out_specs.append(pl.BlockSpec((bt, tl, LANES), lambda i, t, j: (i, t, 0)))
    return pl.pallas_call(
        _inproj_kernel,
        out_shape=out_shapes,
        grid=grid,
        in_specs=[pl.BlockSpec((bt, tl, d), lambda i, t, j: (i, t, 0)),
                  pl.BlockSpec((bt, 1, 1, d), lambda i, t, j: (i, 1, 0, 0)),
                  pl.BlockSpec((bt, 1, 1, d), lambda i, t, j: (i, 0, 0, 0)),
                  pl.BlockSpec((1, d), lambda i, t, j: (0, 0)),
                  pl.BlockSpec((d, IN_TILE), lambda i, t, j: (0, j)),
                  pl.BlockSpec((d, LANES), lambda i, t, j: (0, 0))],
        out_specs=out_specs,
        scratch_shapes=[pltpu.VMEM((bt * tl, d), BF16)],
        compiler_params=_cparams(("arbitrary", "arbitrary", "arbitrary")),
        name="in_proj",
    )(x, mod4, mod4, norm_w.reshape(1, d), w_main, w_small)


def _softmax_step(s, v_blk, m, l, acc):
    m_new = jnp.maximum(m, jnp.max(s, axis=-1, keepdims=True))
    alpha = jnp.exp(m - m_new)
    p = jnp.exp(s - m_new)
    l = alpha * l + jnp.sum(p, axis=-1, keepdims=True)
    acc = alpha * acc + _dot(p, v_blk)
    return m_new, l, acc


def _topk_select(cols, k):
    sels = []
    for n, gn in enumerate(cols):
        rank = jnp.zeros(gn.shape, F32)
        for m_, gm in enumerate(cols):
            if m_ == n:
                continue
            ahead = (gm >= gn) if m_ < n else (gm > gn)
            rank = rank + jnp.where(ahead, 1.0, 0.0)
        sels.append(rank < float(k))
    return sels


def _moba_prompt_kernel(rb_ref, bkt_ref, q_ref, k_ref, v_ref, o_ref, bias_ref):
    h = pl.program_id(1)
    s_len = q_ref.shape[1]
    blk = MOBA_BLOCK
    nb = s_len // blk
    scale = HEAD_DIM ** -0.5

    for t in range(2):
        bkt = bkt_ref[t]
        bias = jnp.zeros((blk, blk), F32)
        for r in range(REL_BUCKETS):
            bias = jnp.where(bkt == r, rb_ref[r, h], bias)
        bias_ref[t] = bias
    bias_far = rb_ref[REL_BUCKETS - 1, h]

    row = lax.broadcasted_iota(I32, (blk, blk), 0)
    col = lax.broadcasted_iota(I32, (blk, blk), 1)
    causal = col <= row

    k_means = [jnp.mean(k_ref[0, n * blk:(n + 1) * blk, :], axis=0, keepdims=True) for n in range(nb)]

    for qb in range(nb):
        q = q_ref[0, qb * blk:(qb + 1) * blk, :]
        qs = (q * scale).astype(BF16)
        if qb > MOBA_TOPK:
            gates = [jnp.sum(q * k_means[n], axis=-1, keepdims=True) for n in range(qb)]
            sels = _topk_select(gates, MOBA_TOPK)
        else:
            sels = [None] * qb
        s = _dot_nt(qs, k_ref[0, qb * blk:(qb + 1) * blk, :]) + bias_ref[0]
        s = jnp.where(causal, s, NEG_INF)
        m = jnp.max(s, axis=-1, keepdims=True)
        p = jnp.exp(s - m)
        l = jnp.sum(p, axis=-1, keepdims=True)
        acc = _dot(p, v_ref[0, qb * blk:(qb + 1) * blk, :])
        for n in range(qb - 1, -1, -1):
            s = _dot_nt(qs, k_ref[0, n * blk:(n + 1) * blk, :])
            s = s + (bias_ref[1] if n == qb - 1 else bias_far)
            if sels[n] is not None:
                s = jnp.where(sels[n], s, NEG_INF)
            m, l, acc = _softmax_step(s, v_ref[0, n * blk:(n + 1) * blk, :], m, l, acc)
        o_ref[0, qb * blk:(qb + 1) * blk, :] = acc / l


def moba_prompt(q, k, v, rel_bias):
    b, s_len, width = q.shape
    assert s_len % MOBA_BLOCK == 0 and width == A_HEADS * HEAD_DIM
    ar = np.arange(MOBA_BLOCK)
    d_loc = ar[:, None] - ar[None, :]
    bkt = np.stack([_rel_bucket_np(d_loc), _rel_bucket_np(d_loc + MOBA_BLOCK)]).astype(np.int32)
    assert int(_rel_bucket_np(np.array([MOBA_BLOCK + 1]))[0]) == REL_BUCKETS - 1
    spec = pl.BlockSpec((1, s_len, HEAD_DIM), lambda i, h: (i, 0, h))
    return pl.pallas_call(
        _moba_prompt_kernel,
        out_shape=jax.ShapeDtypeStruct((b, s_len, width), F32),
        grid=(b, A_HEADS),
        in_specs=[pl.BlockSpec(memory_space=pltpu.SMEM),
                  pl.BlockSpec((2, MOBA_BLOCK, MOBA_BLOCK), lambda i, h: (0, 0, 0)),
                  spec, spec, spec],
        out_specs=spec,
        scratch_shapes=[pltpu.VMEM((2, MOBA_BLOCK, MOBA_BLOCK), F32)],
        compiler_params=_cparams(("arbitrary", "arbitrary")),
        name="moba_prompt",
    )(rel_bias, jnp.asarray(bkt), q, k, v)


def _diag_heads(full, t):
    return jnp.concatenate([full[h * t:(h + 1) * t, h * HEAD_DIM:(h + 1) * HEAD_DIM] for h in range(A_HEADS)],
                           axis=0)


def _moba_sample_kernel(pt_ref, rb_ref, bkt_ref, q_ref, kn_ref, vn_ref, k0_ref, k1_ref, v0_ref, v1_ref, o_ref,
                        qf_ref, qb_ref, bias_ref, m_ref, l_ref, g_ref, acc_ref, *, past_len):
    n = pl.program_id(1)
    nb = pl.num_programs(1)
    t = q_ref.shape[1]
    rows = A_HEADS * t
    width = A_HEADS * HEAD_DIM
    blk = MOBA_BLOCK
    scale = HEAD_DIM ** -0.5
    n_last = past_len // blk - 1

    @pl.when(n == 0)
    def _():
        q8 = jnp.concatenate([q_ref[0]] * A_HEADS, axis=0)
        r_head = lax.broadcasted_iota(I32, (rows, width), 0) // t
        c_head = lax.broadcasted_iota(I32, (rows, width), 1) // HEAD_DIM
        qf = jnp.where(r_head == c_head, q8, 0.0)
        qf_ref[...] = qf
        qb_ref[...] = (qf * scale).astype(BF16)
        for h in range(A_HEADS):
            bkt = bkt_ref[...]
            bias = jnp.zeros((t, blk), F32)
            for r in range(REL_BUCKETS):
                bias = jnp.where(bkt == r, rb_ref[r, h], bias)
            bias_ref[h * t:(h + 1) * t, :] = bias

    def page_rows(ref):
        return jnp.concatenate([ref[0, 0, :, h, :] for h in range(A_HEADS)], axis=1)

    k_blk = jnp.concatenate([page_rows(k0_ref), page_rows(k1_ref)], axis=0)
    v_blk = jnp.concatenate([page_rows(v0_ref), page_rows(v1_ref)], axis=0)
    k_mean = jnp.mean(k_blk, axis=0, keepdims=True)
    g_ref[n] = jnp.sum(qf_ref[...] * k_mean, axis=-1, keepdims=True)
    far_col = jnp.concatenate([jnp.full((t, 1), rb_ref[REL_BUCKETS - 1, h], F32) for h in range(A_HEADS)], axis=0)
    s = _dot_nt(qb_ref[...], k_blk)
    s = s + jnp.where(n == n_last, bias_ref[...], far_col)
    m = jnp.max(s, axis=-1, keepdims=True)
    p = jnp.exp(s - m)
    m_ref[n] = m
    l_ref[n] = jnp.sum(p, axis=-1, keepdims=True)
    acc_ref[n] = _diag_heads(_dot(p, v_blk), t)

    @pl.when(n == nb - 1)
    def _():
        n_blocks = past_len // blk
        gates = [g_ref[i] for i in range(n_blocks)]
        sels = _topk_select(gates, MOBA_TOPK)
        s_loc = _dot_nt(qb_ref[...], kn_ref[0])
        tq = lax.broadcasted_iota(I32, (rows, t), 0) % t
        tk = lax.broadcasted_iota(I32, (rows, t), 1)
        r_head = lax.broadcasted_iota(I32, (rows, t), 0) // t
        bias_loc = jnp.zeros((rows, t), F32)
        for h in range(A_HEADS):
            for d in range(t):
                bias_loc = jnp.where((r_head == h) & (tq - tk == d), rb_ref[d, h], bias_loc)
        s_loc = jnp.where(tk <= tq, s_loc + bias_loc, NEG_INF)
        m_tot = jnp.max(s_loc, axis=-1, keepdims=True)
        for i in range(n_blocks):
            m_tot = jnp.maximum(m_tot, jnp.where(sels[i], m_ref[i], NEG_INF))
        p_loc = jnp.exp(s_loc - m_tot)
        l_tot = jnp.sum(p_loc, axis=-1, keepdims=True)
        acc = _diag_heads(_dot(p_loc, vn_ref[0]), t)
        for i in range(n_blocks):
            w = jnp.where(sels[i], jnp.exp(jnp.where(sels[i], m_ref[i] - m_tot, 0.0)), 0.0)
            l_tot = l_tot + w * l_ref[i]
            acc = acc + w * acc_ref[i]
        out = acc / l_tot
        o_ref[0] = jnp.concatenate([out[h * t:(h + 1) * t, :] for h in range(A_HEADS)], axis=1)


def moba_sample(q, k_new, v_new, cache_k, cache_v, layer, page_table, rel_bias, past_len):
    db, t, width = q.shape
    page = cache_k.shape[2]
    assert MOBA_BLOCK == 2 * page and past_len % MOBA_BLOCK == 0 and t <= REL_BUCKETS // 2
    assert past_len // MOBA_BLOCK >= MOBA_TOPK and t % SUBLANES == 0
    n_blocks = past_len // MOBA_BLOCK
    rows = A_HEADS * t
    d_last = MOBA_BLOCK + np.arange(t)[:, None] - np.arange(MOBA_BLOCK)[None, :]
    bkt = _rel_bucket_np(d_last).astype(np.int32)
    assert int(_rel_bucket_np(np.array([MOBA_BLOCK + 1]))[0]) == REL_BUCKETS - 1
    tok_spec = pl.BlockSpec((1, t, width), lambda i, n, pt: (i, 0, 0))

    def page_spec(j):
        return pl.BlockSpec((1, 1, page, A_HEADS, HEAD_DIM),
                            lambda i, n, pt: (layer, pt[i, 2 * n + j], 0, 0, 0))

    grid_spec = pltpu.PrefetchScalarGridSpec(
        num_scalar_prefetch=1,
        grid=(db, n_blocks),
        in_specs=[pl.BlockSpec(memory_space=pltpu.SMEM),
                  pl.BlockSpec((t, MOBA_BLOCK), lambda i, n, pt: (0, 0)),
                  tok_spec, tok_spec, tok_spec,
                  page_spec(0), page_spec(1), page_spec(0), page_spec(1)],
        out_specs=tok_spec,
        scratch_shapes=[pltpu.VMEM((rows, width), F32),
                        pltpu.VMEM((rows, width), BF16),
                        pltpu.VMEM((rows, MOBA_BLOCK), F32),
                        pltpu.VMEM((n_blocks, rows, 1), F32),
                        pltpu.VMEM((n_blocks, rows, 1), F32),
                        pltpu.VMEM((n_blocks, rows, 1), F32),
                        pltpu.VMEM((n_blocks, rows, HEAD_DIM), F32)],
    )
    return pl.pallas_call(
        functools.partial(_moba_sample_kernel, past_len=past_len),
        out_shape=jax.ShapeDtypeStruct((db, t, width), F32),
        grid_spec=grid_spec,
        compiler_params=_cparams(("arbitrary", "arbitrary")),
        name="moba_sample",
    )(page_table, rel_bias, jnp.asarray(bkt), q, k_new, v_new, cache_k, cache_k, cache_v, cache_v)


INV_BASE = 16


def _unit_lower_inverses(a_mats, ri, ci, size):
    base = min(INV_BASE, size)
    same = ri // base == ci // base
    eye = jnp.where(ri == ci, 1.0, 0.0)
    pws = [jnp.where(same, a, 0.0) for a in a_mats]
    invs = [eye - p for p in pws]
    for _ in range(max(int(math.log2(base)) - 1, 0)):
        pws = [_mm(p, p, "nn", 1) for p in pws]
        invs = [i + _mm(i, p, "nn", 1) for i, p in zip(invs, pws)]
    blk = base
    while blk < size:
        pr = ri // blk
        pc = ci // blk
        join = (pr == pc + 1) & (pr // 2 == pc // 2)
        halves = [_mm(i, jnp.where(join, a, 0.0), "nn", 1) for i, a in zip(invs, a_mats)]
        invs = [i - _mm(hf, i, "nn", 1) for i, hf in zip(invs, halves)]
        blk *= 2
    return invs


def _delta_kernel(x_ref, cw_ref, z_ref, sm_ref, smt_ref, ar_ref, dr_ref, ac_ref, dc_ref, nw_ref, s0_ref, b0_ref,
                  y_ref, sn_ref, tail_ref, s_sc, carry_sc):
    c = pl.program_id(1)
    nc = pl.num_programs(1)
    chunk = x_ref.shape[1]
    dk = HEAD_DIM
    hh = G_HEADS
    part = hh * dk

    @pl.when(c == 0)
    def _():
        s_sc[...] = s0_ref[0]
        carry_sc[...] = b0_ref[0]

    x = x_ref[0]
    xx = jnp.concatenate([carry_sc[...], x], axis=0)
    cw = cw_ref[...]
    y = x * cw[CONV_W - 1:CONV_W, :]
    for s in range(1, CONV_W):
        y = y + xx[SUBLANES - s:SUBLANES - s + chunk, :] * cw[CONV_W - 1 - s:CONV_W - s, :]
    carry_sc[...] = xx[chunk:chunk + SUBLANES, :]
    qkv = _silu(y)

    def softplus(t):
        return jnp.maximum(t, 0.0) + jnp.log(1.0 + jnp.exp(-jnp.abs(t)))

    sm = sm_ref[0]
    beta_all = jax.nn.sigmoid(sm)
    g_cols = -jnp.exp(ar_ref[...]) * softplus(sm + dr_ref[...])
    g_rows = -jnp.exp(ac_ref[...]) * softplus(smt_ref[0] + dc_ref[...])
    ri = lax.broadcasted_iota(I32, (chunk, chunk), 0)
    ci = lax.broadcasted_iota(I32, (chunk, chunk), 1)
    incl = ci <= ri
    strict = ci < ri
    gc_cols = _dot_exact(jnp.where(incl, 1.0, 0.0), g_cols)
    gc_rows = _dot_exact(g_rows, jnp.where(ri <= ci, 1.0, 0.0))
    nw = nw_ref[...]

    heads = range(hh)
    qs, ks, vs, betas, gcs, dmasks = [], [], [], [], [], []
    for h in heads:
        q = qkv[:, h * dk:(h + 1) * dk]
        k = qkv[:, part + h * dk:part + (h + 1) * dk]
        qs.append(q * lax.rsqrt(jnp.sum(q * q, axis=-1, keepdims=True) + EPS) * (dk ** -0.5))
        ks.append(k * lax.rsqrt(jnp.sum(k * k, axis=-1, keepdims=True) + EPS))
        vs.append(qkv[:, 2 * part + h * dk:2 * part + (h + 1) * dk])
        betas.append(beta_all[:, h:h + 1])
        gc_c = gc_cols[:, hh + h:hh + h + 1]
        gc_r = gc_rows[hh + h:hh + h + 1, :]
        gcs.append(gc_c)
        dmasks.append(jnp.exp(jnp.where(incl, gc_c - gc_r, NEG_INF)))

    kbs = [k * b for k, b in zip(ks, betas)]
    a_mats = [jnp.where(strict, _mm(kb, k, "nt", 1) * dm, 0.0) for kb, k, dm in zip(kbs, ks, dmasks)]
    t_invs = _unit_lower_inverses(a_mats, ri, ci, chunk)
    egs = [jnp.exp(g) for g in gcs]
    uws = [_mm(t, jnp.concatenate([v * b, kb * eg], axis=1), "nn", 1)
           for t, v, b, kb, eg in zip(t_invs, vs, betas, kbs, egs)]
    s_mats = [s_sc[h] for h in heads]
    v_news = [uw[:, :dk] - _mm(uw[:, dk:], s, "nn", 1) for uw, s in zip(uws, s_mats)]
    attns = [_mm(q, k, "nt", 1) * dm for q, k, dm in zip(qs, ks, dmasks)]
    outs = [_mm(q * eg, s, "nn", 1) + _mm(at, vn, "nn", 1)
            for q, eg, s, at, vn in zip(qs, egs, s_mats, attns, v_news)]
    for h in heads:
        g_last = gcs[h][chunk - 1:chunk, :]
        s_sc[h] = s_mats[h] * jnp.exp(g_last) + _mm(ks[h] * jnp.exp(g_last - gcs[h]), v_news[h], "tn", 1)
    for h in heads:
        o = outs[h]
        o_n = o * lax.rsqrt(jnp.mean(o * o, axis=-1, keepdims=True) + EPS) * nw
        y_ref[0, :, h * dk:(h + 1) * dk] = o_n * _silu(z_ref[0, :, h * dk:(h + 1) * dk])

    @pl.when(c == nc - 1)
    def _():
        sn_ref[0] = s_sc[...]
        tail_ref[0] = carry_sc[...]


def delta_rule(xg, z, small, small_t, conv_w, a_log, dt_bias, delta_norm, s0, buf0, chunk):
    b, l, cw_ch = xg.shape
    hh = G_HEADS
    assert l % chunk == 0 and cw_ch == 3 * hh * HEAD_DIM and l >= CONV_W - 1
    nc = l // chunk
    width = hh * HEAD_DIM
    a_row = jnp.pad(a_log.reshape(1, hh), ((0, 0), (hh, LANES - 2 * hh)))
    d_row = jnp.pad(dt_bias.reshape(1, hh), ((0, 0), (hh, LANES - 2 * hh)))
    a_col = jnp.pad(a_log.reshape(hh, 1), ((hh, 0), (0, 0)))
    d_col = jnp.pad(dt_bias.reshape(hh, 1), ((hh, 0), (0, 0)))

    def const(shape):
        return pl.BlockSpec(shape, lambda i, c: (0,) * len(shape))

    return pl.pallas_call(
        _delta_kernel,
        out_shape=[jax.ShapeDtypeStruct((b, l, width), F32),
                   jax.ShapeDtypeStruct((b, hh, HEAD_DIM, HEAD_DIM), F32),
                   jax.ShapeDtypeStruct((b, SUBLANES, cw_ch), F32)],
        grid=(b, nc),
        in_specs=[pl.BlockSpec((1, chunk, cw_ch), lambda i, c: (i, c, 0)),
                  const((CONV_W, cw_ch)),
                  pl.BlockSpec((1, chunk, width), lambda i, c: (i, c, 0)),
                  pl.BlockSpec((1, chunk, LANES), lambda i, c: (i, c, 0)),
                  pl.BlockSpec((1, 2 * hh, chunk), lambda i, c: (i, 0, c)),
                  const((1, LANES)), const((1, LANES)), const((2 * hh, 1)), const((2 * hh, 1)),
                  const((1, HEAD_DIM)),
                  pl.BlockSpec((1, hh, HEAD_DIM, HEAD_DIM), lambda i, c: (i, 0, 0, 0)),
                  pl.BlockSpec((1, SUBLANES, cw_ch), lambda i, c: (i, 0, 0))],
        out_specs=[pl.BlockSpec((1, chunk, width), lambda i, c: (i, c, 0)),
                   pl.BlockSpec((1, hh, HEAD_DIM, HEAD_DIM), lambda i, c: (i, 0, 0, 0)),
                   pl.BlockSpec((1, SUBLANES, cw_ch), lambda i, c: (i, 0, 0))],
        scratch_shapes=[pltpu.VMEM((hh, HEAD_DIM, HEAD_DIM), F32),
                        pltpu.VMEM((SUBLANES, cw_ch), F32)],
        compiler_params=_cparams(("arbitrary", "arbitrary")),
        name="delta_rule",
    )(xg, conv_w, z, small, small_t, a_row, d_row, a_col, d_col, delta_norm.reshape(1, HEAD_DIM), s0, buf0)


def _post_kernel(ya_ref, yb_ref, ga_ref, gb_ref, x_ref, gm_ref, scf_ref, shf_ref, wa_ref, wb_ref, wo_ref,
                 nw_ref, rw_ref, x1_ref, h2_ref, lg_ref):
    bt, tl, d = x_ref.shape
    rows = bt * tl
    ya = ya_ref[...].reshape(rows, -1)
    yb = yb_ref[...].reshape(rows, -1)
    merged = (jax.nn.sigmoid(ga_ref[...].reshape(rows, d)) * _dot(ya, wa_ref[...])
              + jax.nn.sigmoid(gb_ref[...].reshape(rows, d)) * _dot(yb, wb_ref[...]))
    mix = _dot(merged, wo_ref[...]).reshape(bt, tl, d)
    x1 = x_ref[...] + gm_ref[:, 0] * mix
    x1_ref[...] = x1
    y = x1 * lax.rsqrt(jnp.mean(x1 * x1, axis=-1, keepdims=True) + EPS) * nw_ref[...]
    h2 = y * (1.0 + scf_ref[:, 0]) + shf_ref[:, 0]
    h2_ref[...] = h2
    lg_ref[...] = _dot_exact(h2.reshape(rows, d), rw_ref[...]).reshape(bt, tl, -1)


def post_mix(y_a, y_b, gate_a, gate_b, x, mod4, w_a, w_b, w_o, norm_w, router_w, bt, tl):
    b, l, d = x.shape
    wa_in = y_a.shape[-1]
    wb_in = y_b.shape[-1]
    n_e = router_w.shape[1]

    def tok(width):
        return pl.BlockSpec((bt, tl, width), lambda i, t: (i, t, 0))

    def modspec(idx):
        return pl.BlockSpec((bt, 1, 1, d), lambda i, t, idx=idx: (i, idx, 0, 0))

    def const(shape):
        return pl.BlockSpec(shape, lambda i, t: (0, 0), pipeline_mode=pl.Buffered(1))

    return pl.pallas_call(
        _post_kernel,
        out_shape=[jax.ShapeDtypeStruct((b, l, d), F32), jax.ShapeDtypeStruct((b, l, d), F32),
                   jax.ShapeDtypeStruct((b, l, n_e), F32)],
        grid=(b // bt, l // tl),
        in_specs=[tok(wa_in), tok(wb_in), tok(d), tok(d), tok(d), modspec(2), modspec(4), modspec(3),
                  const((wa_in, d)), const((wb_in, d)), const((d, d)), const((1, d)), const((d, n_e))],
        out_specs=[tok(d), tok(d), tok(n_e)],
        compiler_params=_cparams(("arbitrary", "arbitrary")),
        name="post_mix",
    )(y_a, y_b, gate_a, gate_b, x, mod4, mod4, mod4, w_a, w_b, w_o, norm_w.reshape(1, d), router_w)


def _route_kernel(lg_ref, rb_ref, e_ref, w_ref, p_ref, cnt_ref, run_sc, *, n_e):
    i = pl.program_id(0)
    tm, width = lg_ref.shape
    per_group = n_e // N_GROUPS

    @pl.when(i == 0)
    def _():
        run_sc[...] = jnp.zeros_like(run_sc)

    lane = lax.broadcasted_iota(I32, (tm, width), 1)
    scores = jnp.where(lane < n_e, jax.nn.sigmoid(lg_ref[...]), 0.0)
    biased = jnp.where(lane < n_e, scores + rb_ref[...], NEG_INF)
    grp = lane // per_group

    def first_argmax(vals):
        mx = jnp.max(vals, axis=-1, keepdims=True)
        idx = jnp.min(jnp.where(vals == mx, lane, width), axis=-1, keepdims=True)
        return mx, idx

    group_scores = []
    for g in range(N_GROUPS):
        mg = jnp.where(grp == g, biased, NEG_INF)
        m1, i1 = first_argmax(mg)
        m2 = jnp.max(jnp.where(lane == i1, NEG_INF, mg), axis=-1, keepdims=True)
        group_scores.append(m1 + m2)
    keep = _topk_select(group_scores, TOPK_GROUPS)
    expert_keep = jnp.zeros((tm, width), jnp.bool_)
    for g in range(N_GROUPS):
        expert_keep = expert_keep | ((grp == g) & keep[g])
    masked = jnp.where(expert_keep, biased, NEG_INF)

    sel = jnp.zeros((tm, width), jnp.bool_)
    idxs = []
    for _ in range(TOP_K):
        _, idx = first_argmax(masked)
        hit = lane == idx
        sel = sel | hit
        masked = jnp.where(hit, NEG_INF, masked)
        idxs.append(idx)
    sel_f = jnp.where(sel, 1.0, 0.0)
    top_w = scores * sel_f
    top_w = top_w / jnp.sum(top_w, axis=-1, keepdims=True) * ROUTED_SCALE

    ri = lax.broadcasted_iota(I32, (tm, tm), 0)
    ci = lax.broadcasted_iota(I32, (tm, tm), 1)
    before = _dot(jnp.where(ci < ri, 1.0, 0.0), sel_f) + run_sc[...]
    run_sc[...] = run_sc[...] + jnp.sum(sel_f, axis=0, keepdims=True)
    cnt_ref[...] = run_sc[...].astype(I32)

    lane_o = lax.broadcasted_iota(I32, (tm, LANES), 1)
    e_out = jnp.zeros((tm, LANES), I32)
    w_out = jnp.zeros((tm, LANES), F32)
    p_out = jnp.zeros((tm, LANES), I32)
    for kk, idx in enumerate(idxs):
        hit = lane == idx
        wk = jnp.sum(jnp.where(hit, top_w, 0.0), axis=-1, keepdims=True)
        pk = jnp.sum(jnp.where(hit, before, 0.0), axis=-1, keepdims=True).astype(I32)
        e_out = jnp.where(lane_o == kk, idx, e_out)
        w_out = jnp.where(lane_o == kk, wk, w_out)
        p_out = jnp.where(lane_o == kk, pk, p_out)
    e_ref[...] = e_out
    w_ref[...] = w_out
    p_ref[...] = p_out


def route(logits, router_bias, n_e, tm):
    n, width = logits.shape
    assert n % tm == 0 and width == LANES and n_e <= LANES
    tok = pl.BlockSpec((tm, LANES), lambda i: (i, 0))
    one = pl.BlockSpec((1, LANES), lambda i: (0, 0))
    bias = jnp.pad(router_bias.reshape(1, n_e), ((0, 0), (0, LANES - n_e)))
    return pl.pallas_call(
        functools.partial(_route_kernel, n_e=n_e),
        out_shape=[jax.ShapeDtypeStruct((n, LANES), I32), jax.ShapeDtypeStruct((n, LANES), F32),
                   jax.ShapeDtypeStruct((n, LANES), I32), jax.ShapeDtypeStruct((1, LANES), I32)],
        grid=(n // tm,),
        in_specs=[tok, one],
        out_specs=[tok, tok, tok, one],
        scratch_shapes=[pltpu.VMEM((1, LANES), F32)],
        compiler_params=_cparams(("arbitrary",)),
        name="route",
    )(logits, bias)


def _row_copy(src_ref, dst_ref, src_row, dst_row, sem):
    return pltpu.make_async_copy(src_ref.at[pl.ds(src_row, 1)], dst_ref.at[pl.ds(dst_row, 1)], sem)


def _dispatch_kernel(dest_ref, h_ref, hs_in_ref, hs_ref, sem):
    del hs_in_ref
    tm = h_ref.shape[0]

    def body(t, carry):
        for kk in range(TOP_K):
            _row_copy(h_ref, hs_ref, t, dest_ref[0, 0, t * TOP_K + kk], sem).start()
        return carry

    def drain(t, carry):
        for kk in range(TOP_K):
            _row_copy(h_ref, hs_ref, t, dest_ref[0, 0, t * TOP_K + kk], sem).wait()
        return carry

    lax.fori_loop(0, tm, body, 0)
    lax.fori_loop(0, tm, drain, 0)


def dispatch(h_all, dest, n_rows, tm):
    n, d = h_all.shape
    assert n % tm == 0
    dest3 = dest.reshape(n // tm, 1, tm * TOP_K)
    zeros = jnp.zeros((n_rows, d), F32)
    return pl.pallas_call(
        _dispatch_kernel,
        out_shape=jax.ShapeDtypeStruct((n_rows, d), F32),
        grid=(n // tm,),
        in_specs=[pl.BlockSpec((1, 1, tm * TOP_K), lambda i: (i, 0, 0), memory_space=pltpu.SMEM),
                  pl.BlockSpec((tm, d), lambda i: (i, 0)),
                  pl.BlockSpec(memory_space=pl.ANY)],
        out_specs=pl.BlockSpec(memory_space=pl.ANY),
        scratch_shapes=[pltpu.SemaphoreType.DMA(())],
        input_output_aliases={2: 0},
        compiler_params=_cparams(("arbitrary",)),
        name="dispatch",
    )(dest3, h_all, zeros)


def _experts_kernel(be_ref, bf_ref, nu_ref, x_ref, wg_ref, wu_ref, wd_ref, y_ref, wg_sc, wu_sc, wd_sc):
    i = pl.program_id(0)

    @pl.when(bf_ref[i] == 1)
    def _():
        wg_sc[...] = wg_ref[0].astype(BF16)
        wu_sc[...] = wu_ref[0].astype(BF16)
        wd_sc[...] = wd_ref[0].astype(BF16)

    @pl.when(i < nu_ref[0])
    def _():
        x = x_ref[...].astype(BF16)
        g = jnp.dot(x, wg_sc[...], preferred_element_type=F32)
        u = jnp.dot(x, wu_sc[...], preferred_element_type=F32)
        y_ref[...] = jnp.dot((_silu(g) * u).astype(BF16), wd_sc[...], preferred_element_type=F32)

    @pl.when(i >= nu_ref[0])
    def _():
        y_ref[...] = jnp.zeros_like(y_ref)


def experts(h_sorted, blk_e, blk_first, n_used, w_gate, w_up, w_down):
    n_rows, d = h_sorted.shape
    n_blocks = n_rows // EXPERT_ROWS
    de = w_gate.shape[-1]
    grid_spec = pltpu.PrefetchScalarGridSpec(
        num_scalar_prefetch=3,
        grid=(n_blocks,),
        in_specs=[pl.BlockSpec((EXPERT_ROWS, d), lambda i, be, bf, nu: (i, 0)),
                  pl.BlockSpec((1, d, de), lambda i, be, bf, nu: (be[i], 0, 0)),
                  pl.BlockSpec((1, d, de), lambda i, be, bf, nu: (be[i], 0, 0)),
                  pl.BlockSpec((1, de, d), lambda i, be, bf, nu: (be[i], 0, 0))],
        out_specs=pl.BlockSpec((EXPERT_ROWS, d), lambda i, be, bf, nu: (i, 0)),
        scratch_shapes=[pltpu.VMEM((d, de), BF16), pltpu.VMEM((d, de), BF16), pltpu.VMEM((de, d), BF16)],
    )
    return pl.pallas_call(
        _experts_kernel,
        out_shape=jax.ShapeDtypeStruct((n_rows, d), F32),
        grid_spec=grid_spec,
        compiler_params=_cparams(("arbitrary",)),
        name="experts",
    )(blk_e, blk_first, n_used, h_sorted, w_gate, w_up, w_down)


def _combine_kernel(dest_ref, x1_ref, h2_ref, gf_ref, w_ref, wg_ref, wu_ref, wd_ref, nw_ref, yr_ref, o_ref,
                    gbuf, sem):
    bt, tl, d = x1_ref.shape
    tm = bt * tl

    def body(t, carry):
        for kk in range(TOP_K):
            _row_copy(yr_ref, gbuf.at[kk], dest_ref[0, 0, t * TOP_K + kk], t, sem).start()
        return carry

    lax.fori_loop(0, tm, body, 0)
    h2 = h2_ref[...].reshape(tm, d).astype(BF16)
    g = jnp.dot(h2, wg_ref[...], preferred_element_type=F32)
    u = jnp.dot(h2, wu_ref[...], preferred_element_type=F32)
    y = jnp.dot((_silu(g) * u).astype(BF16), wd_ref[...], preferred_element_type=F32)
    wts = w_ref[...]

    def drain(t, carry):
        for kk in range(TOP_K):
            _row_copy(yr_ref, gbuf.at[kk], dest_ref[0, 0, t * TOP_K + kk], t, sem).wait()
        return carry

    lax.fori_loop(0, tm, drain, 0)
    for kk in range(TOP_K):
        y = y + gbuf[kk] * wts[:, kk:kk + 1]
    x2 = x1_ref[...] + gf_ref[:, 0] * y.reshape(bt, tl, d)
    o_ref[...] = x2 * lax.rsqrt(jnp.mean(x2 * x2, axis=-1, keepdims=True) + EPS) * nw_ref[...]


def combine(x1, h2, mod4, sel_w, dest, y_rows, w_gs, w_us, w_ds, norm_final, bt, tl):
    b, l, d = x1.shape
    tm = bt * tl
    n = b * l
    ds = w_gs.shape[1]
    n_t = l // tl
    dest3 = dest.reshape(n // tm, 1, tm * TOP_K)

    def const(shape):
        return pl.BlockSpec(shape, lambda i, t: (0, 0), pipeline_mode=pl.Buffered(1))

    return pl.pallas_call(
        _combine_kernel,
        out_shape=jax.ShapeDtypeStruct((b, l, d), F32),
        grid=(b // bt, n_t),
        in_specs=[pl.BlockSpec((1, 1, tm * TOP_K), lambda i, t: (i * n_t + t, 0, 0), memory_space=pltpu.SMEM),
                  pl.BlockSpec((bt, tl, d), lambda i, t: (i, t, 0)),
                  pl.BlockSpec((bt, tl, d), lambda i, t: (i, t, 0)),
                  pl.BlockSpec((bt, 1, 1, d), lambda i, t: (i, 5, 0, 0)),
                  pl.BlockSpec((tm, LANES), lambda i, t: (i * n_t + t, 0)),
                  const((d, ds)), const((d, ds)), const((ds, d)), const((1, d)),
                  pl.BlockSpec(memory_space=pl.ANY)],
        out_specs=pl.BlockSpec((bt, tl, d), lambda i, t: (i, t, 0)),
        scratch_shapes=[pltpu.VMEM((TOP_K, tm, d), F32), pltpu.SemaphoreType.DMA(())],
        compiler_params=_cparams(("arbitrary", "arbitrary")),
        name="combine",
    )(dest3, x1, h2, mod4, sel_w, w_gs, w_us, w_ds, norm_final.reshape(1, d), y_rows)


def _moe(h2_p, h2_s, x1_p, x1_s, lg_p, lg_s, mod_p, mod_s, router_bias, w_gate_e, w_up_e, w_down_e,
         w_gs, w_us, w_ds, norm_final, tiles_p, tiles_s):
    d = h2_p.shape[-1]
    n_p = h2_p.shape[0] * h2_p.shape[1]
    n_s = h2_s.shape[0] * h2_s.shape[1]
    n = n_p + n_s
    n_e = w_gate_e.shape[0]
    h_all = jnp.concatenate([h2_p.reshape(n_p, d), h2_s.reshape(n_s, d)], axis=0)
    logits = jnp.concatenate([lg_p.reshape(n_p, LANES), lg_s.reshape(n_s, LANES)], axis=0)
    sel_e, sel_w, sel_pos, counts = route(logits, router_bias, n_e, 512)
    counts = counts[0, :n_e]
    padded = (counts + EXPERT_ROWS - 1) // EXPERT_ROWS * EXPERT_ROWS
    pad_ends = jnp.cumsum(padded)
    pad_starts = pad_ends - padded
    n_blocks = -(-(n * TOP_K + n_e * (EXPERT_ROWS - 1)) // EXPERT_ROWS)
    n_rows = n_blocks * EXPERT_ROWS
    dest = pad_starts[sel_e[:, :TOP_K]] + sel_pos[:, :TOP_K]
    blk_start = jnp.arange(n_blocks, dtype=I32) * EXPERT_ROWS
    n_used = (pad_ends[-1] // EXPERT_ROWS).astype(I32)
    blk_e = jnp.minimum(jnp.sum((blk_start[:, None] >= pad_ends[None, :]).astype(I32), axis=1), n_e - 1)
    last_used_e = blk_e[jnp.maximum(n_used - 1, 0)]
    blk_e = jnp.where(jnp.arange(n_blocks) < n_used, blk_e, last_used_e)
    blk_first = jnp.concatenate([jnp.ones((1,), I32), (blk_e[1:] != blk_e[:-1]).astype(I32)])
    h_sorted = dispatch(h_all, dest, n_rows, 512)
    y_rows = experts(h_sorted, blk_e, blk_first, n_used.reshape(1), w_gate_e, w_up_e, w_down_e)
    y_p = combine(x1_p, h2_p, mod_p, sel_w[:n_p], dest[:n_p], y_rows, w_gs, w_us, w_ds, norm_final, *tiles_p)
    y_s = combine(x1_s, h2_s, mod_s, sel_w[n_p:], dest[n_p:], y_rows, w_gs, w_us, w_ds, norm_final, *tiles_s)
    return y_p, y_s


def kernel(x_prompt, x_sample, cache_k, cache_v, state_delta, state_conv, page_table, c_prompt, c_sample,
           w_ada, b_ada, norm_mix, w_in, conv_w, a_log, dt_bias, delta_norm, w_branch_a, w_branch_b, w_out,
           rel_bias, norm_ffn, router_w, router_bias, w_gate_e, w_up_e, w_down_e, w_gate_s, w_up_s, w_down_s,
           norm_final):
    depth = w_ada.shape[0]
    assert depth == 1, "the final norm is fused into the layer's last stage"
    n_b, seq, d = x_prompt.shape
    d_b, d_seq, _ = x_sample.shape
    n_pages = page_table.shape[1]
    page = cache_k.shape[2]
    past_len = n_pages * page
    a_width = A_HEADS * HEAD_DIM
    g_width = G_HEADS * HEAD_DIM
    conv_ch = 3 * g_width
    l = 0

    off_small = 3 * a_width + conv_ch + g_width
    w_l = w_in[l]
    w_main = jnp.concatenate([w_l[:, :off_small], w_l[:, off_small + 2 * G_HEADS:]], axis=1).astype(BF16)
    w_small = jnp.pad(w_l[:, off_small:off_small + 2 * G_HEADS], ((0, 0), (0, LANES - 2 * G_HEADS))).astype(BF16)
    assert w_main.shape[1] == IN_COL_TILES * IN_TILE

    c_all = jnp.concatenate([c_prompt, c_sample], axis=0)
    mod = ada_mod(c_all, w_ada[l], b_ada[l])
    mod_p = mod[:n_b].reshape(n_b, N_ADA, 1, d)
    mod_s = mod[n_b:].reshape(d_b, N_ADA, 1, d)

    tiles_p = (1, 512)
    tiles_s = (512 // d_seq, d_seq)
    wa = w_branch_a[l].astype(BF16)
    wb = w_branch_b[l].astype(BF16)
    wo = w_out[l].astype(BF16)
    rw = jnp.pad(router_w[l], ((0, 0), (0, LANES - router_w.shape[-1])))

    def mixer(x, mod4, tiles, attend, s0, buf0, chunk):
        a_q, a_k, a_v, g_qkv, g_z, gate_a, gate_b, small = in_proj(x, mod4, norm_mix[l], w_main, w_small, *tiles)
        y_a = attend(a_q, a_k, a_v)
        small_t = jnp.swapaxes(small[..., :2 * G_HEADS], 1, 2)
        y_b, s_new, tail = delta_rule(g_qkv, g_z, small, small_t, conv_w[l], a_log[l], dt_bias[l],
                                      delta_norm[l], s0, buf0, chunk)
        post_tiles = (tiles[0], tiles[1] // 2) if tiles[0] == 1 else (tiles[0] // 2, tiles[1])
        x1, h2, lg = post_mix(y_a, y_b, gate_a, gate_b, x, mod4, wa, wb, wo, norm_ffn[l], rw, *post_tiles)
        return x1, h2, lg, a_k, a_v, s_new, tail[:, SUBLANES - (CONV_W - 1):]

    s0_p = jnp.zeros((n_b, G_HEADS, HEAD_DIM, HEAD_DIM), F32)
    buf0_p = jnp.zeros((n_b, SUBLANES, conv_ch), F32)
    buf0_s = jnp.pad(state_conv[l], ((0, 0), (SUBLANES - (CONV_W - 1), 0), (0, 0)))

    x1_p, h2_p, lg_p, k_p, v_p, d_p, cv_p = mixer(
        x_prompt, mod_p, tiles_p, functools.partial(moba_prompt, rel_bias=rel_bias), s0_p, buf0_p, DELTA_CHUNK)
    x1_s, h2_s, lg_s, k_s, v_s, d_s, cv_s = mixer(
        x_sample, mod_s, tiles_s,
        functools.partial(moba_sample, cache_k=cache_k, cache_v=cache_v, layer=l, page_table=page_table,
                          rel_bias=rel_bias, past_len=past_len),
        state_delta[l], buf0_s, d_seq)

    comb_p = (1, 128)
    comb_s = (128 // d_seq, d_seq)
    y_p, y_s = _moe(h2_p, h2_s, x1_p, x1_s, lg_p, lg_s, mod_p, mod_s, router_bias[l],
                    w_gate_e[l], w_up_e[l], w_down_e[l],
                    w_gate_s[l].astype(BF16), w_up_s[l].astype(BF16), w_down_s[l].astype(BF16),
                    norm_final, comb_p, comb_s)

    def heads(t, n_heads):
        return t.reshape(t.shape[:-1] + (n_heads, HEAD_DIM))[None]

    return (y_p, y_s, heads(k_p, A_HEADS), heads(v_p, A_HEADS), heads(k_s, A_HEADS), heads(v_s, A_HEADS),
            d_p[None], cv_p[None], d_s[None], cv_s[None])
```

```python
import functools
import math

import numpy as np
import jax
import jax.numpy as jnp
from jax import lax
from jax.experimental import pallas as pl
from jax.experimental.pallas import tpu as pltpu

F32 = jnp.float32
BF16 = jnp.bfloat16
I32 = jnp.int32
EPS = 1e-6
NEG_INF = float("-inf")

A_HEADS = 8
G_HEADS = 8
MOBA_BLOCK = 256
MOBA_TOPK = 3
REL_BUCKETS = 32
REL_MAX_DIST = 128
CONV_W = 4
N_GROUPS = 8
TOPK_GROUPS = 4
TOP_K = 6
ROUTED_SCALE = 2.5
N_ADA = 6
HEAD_DIM = 128

LANES = 128
SUBLANES = 8
VMEM_LIMIT = 56 * 1024 * 1024
EXPERT_ROWS = 256
DELTA_CHUNK = 128


def _cparams(sem, vmem=VMEM_LIMIT):
    return pltpu.CompilerParams(dimension_semantics=sem, vmem_limit_bytes=vmem)


def _dot(a, b):
    return jnp.dot(a.astype(BF16), b.astype(BF16), preferred_element_type=F32)


def _dot_nt(a, b):
    return lax.dot_general(a.astype(BF16), b.astype(BF16), (((1,), (1,)), ((), ())),
                           preferred_element_type=F32)


def _dot_tn(a, b):
    return lax.dot_general(a.astype(BF16), b.astype(BF16), (((0,), (0,)), ((), ())),
                           preferred_element_type=F32)


def _dot_exact(a, b):
    return jnp.dot(a, b, precision=lax.Precision.HIGHEST, preferred_element_type=F32)


_DIMS = {"nn": (((1,), (0,)), ((), ())), "nt": (((1,), (1,)), ((), ())), "tn": (((0,), (0,)), ((), ()))}

def _mm(a, b, form, passes):
    dims = _DIMS[form]
    a_hi = a.astype(BF16)
    b_hi = b.astype(BF16)
    out = lax.dot_general(a_hi, b_hi, dims, preferred_element_type=F32)
    if passes == 3:
        a_lo = (a - a_hi.astype(F32)).astype(BF16)
        b_lo = (b - b_hi.astype(F32)).astype(BF16)
        out = out + (lax.dot_general(a_hi, b_lo, dims, preferred_element_type=F32)
                     + lax.dot_general(a_lo, b_hi, dims, preferred_element_type=F32))
    return out


def _pack_pairs(x):
    w = x.shape[-1] // 2
    lo = lax.bitcast_convert_type(x[:, :w].astype(BF16).astype(F32), jnp.uint32)
    hi = lax.bitcast_convert_type(x[:, w:].astype(BF16).astype(F32), jnp.uint32)
    return (lo >> 16) | (hi & jnp.uint32(0xFFFF0000))


def _unpack_pairs(u):
    lo = lax.bitcast_convert_type(u << 16, F32)
    hi = lax.bitcast_convert_type(u & jnp.uint32(0xFFFF0000), F32)
    return lo, hi


def _silu(x):
    return x * jax.nn.sigmoid(x)


def _rel_bucket_np(dist):
    n = np.maximum(dist, 0)
    max_exact = REL_BUCKETS // 2
    nf = np.maximum(n, 1).astype(np.float32)
    large = max_exact + (np.log(nf / np.float32(max_exact)) / np.float32(math.log(REL_MAX_DIST / max_exact))
                         * np.float32(REL_BUCKETS - max_exact)).astype(np.int32)
    return np.where(n < max_exact, n, np.minimum(large, REL_BUCKETS - 1)).astype(np.int32)


def _ada_kernel(c_ref, w_ref, b_ref, o_ref):
    o_ref[...] = _dot(_silu(c_ref[...]), w_ref[...]) + b_ref[...]


def ada_mod(c_all, w_ada, b_ada):
    rows, d = c_all.shape
    n = w_ada.shape[1]
    tn = 1024
    return pl.pallas_call(
        _ada_kernel,
        out_shape=jax.ShapeDtypeStruct((rows, n), F32),
        grid=(n // tn,),
        in_specs=[pl.BlockSpec((rows, d), lambda j: (0, 0)),
                  pl.BlockSpec((d, tn), lambda j: (0, j)),
                  pl.BlockSpec((1, tn), lambda j: (0, j))],
        out_specs=pl.BlockSpec((rows, tn), lambda j: (0, j)),
        compiler_params=_cparams(("arbitrary",)),
        name="ada_mod",
    )(c_all, w_ada, b_ada.reshape(1, n))


IN_TILE = 1024
IN_GROUPS = (("a_q", 0, 1), ("a_k", 1, 1), ("a_v", 2, 1), ("g_qkv", 3, 3), ("g_z", 6, 1),
             ("gate_a", 7, 2), ("gate_b", 9, 2))
IN_COL_TILES = 11


def _inproj_kernel(x_ref, sc_ref, sh_ref, nw_ref, w_ref, ws_ref, *rest):
    out_refs = rest[:len(IN_GROUPS)]
    small_ref = rest[len(IN_GROUPS)]
    h_ref = rest[len(IN_GROUPS) + 1]
    bt, tl, d = x_ref.shape
    j = pl.program_id(2)

    @pl.when(j == 0)
    def _():
        x = x_ref[...]
        y = x * lax.rsqrt(jnp.mean(x * x, axis=-1, keepdims=True) + EPS) * nw_ref[...]
        h = y * (1.0 + sc_ref[:, 0]) + sh_ref[:, 0]
        h2 = h.reshape(bt * tl, d).astype(BF16)
        h_ref[...] = h2
        small_ref[...] = jnp.dot(h2, ws_ref[...], preferred_element_type=F32).reshape(bt, tl, LANES)

    res = jnp.dot(h_ref[...], w_ref[...], preferred_element_type=F32).reshape(bt, tl, IN_TILE)
    for o_ref, (_, j0, nj) in zip(out_refs, IN_GROUPS):
        @pl.when((j >= j0) & (j < j0 + nj))
        def _(o_ref=o_ref):
            o_ref[...] = res


def in_proj(x, mod4, norm_w, w_main, w_small, bt, tl):
    b, l, d = x.shape
    grid = (b // bt, l // tl, IN_COL_TILES)
    out_shapes, out_specs = [], []
    for _, j0, nj in IN_GROUPS:
        out_shapes.append(jax.ShapeDtypeStruct((b, l, nj * IN_TILE), F32))
        out_specs.append(pl.BlockSpec((bt, tl, IN_TILE),
                                      lambda i, t, j, j0=j0, nj=nj: (i, t, jnp.clip(j - j0, 0, nj - 1))))
    out_shapes.append(jax.ShapeDtypeStruct((b, l, LANES), F32))
    out_specs.append(pl.BlockSpec((bt, tl, LANES), lambda i, t, j: (i, t, 0)))
    return pl.pallas_call(
        _inproj_kernel,
        out_shape=out_shapes,
        grid=grid,
        in_specs=[pl.BlockSpec((bt, tl, d), lambda i, t, j: (i, t, 0)),
                  pl.BlockSpec((bt, 1, 1, d), lambda i, t, j: (i, 1, 0, 0)),
                  pl.BlockSpec((bt, 1, 1, d), lambda i, t, j: (i, 0, 0, 0)),
                  pl.BlockSpec((1, d), lambda i, t, j: (0, 0)),
                  pl.BlockSpec((d, IN_TILE), lambda i, t, j: (0, j)),
                  pl.BlockSpec((d, LANES), lambda i, t, j: (0, 0))],
        out_specs=out_specs,
        scratch_shapes=[pltpu.VMEM((bt * tl, d), BF16)],
        compiler_params=_cparams(("arbitrary", "arbitrary", "arbitrary")),
        name="in_proj",
    )(x, mod4, mod4, norm_w.reshape(1, d), w_main, w_small)


def _softmax_step(s, v_blk, m, l, acc):
    m_new = jnp.maximum(m, jnp.max(s, axis=-1, keepdims=True))
    alpha = jnp.exp(m - m_new)
    p = jnp.exp(s - m_new)
    l = alpha * l + jnp.sum(p, axis=-1, keepdims=True)
    acc = alpha * acc + _dot(p, v_blk)
    return m_new, l, acc


def _topk_select(cols, k):
    sels = []
    for n, gn in enumerate(cols):
        rank = jnp.zeros(gn.shape, F32)
        for m_, gm in enumerate(cols):
            if m_ == n:
                continue
            ahead = (gm >= gn) if m_ < n else (gm > gn)
            rank = rank + jnp.where(ahead, 1.0, 0.0)
        sels.append(rank < float(k))
    return sels


def _moba_prompt_kernel(rb_ref, bkt_ref, q_ref, k_ref, v_ref, o_ref, bias_ref):
    h = pl.program_id(1)
    s_len = q_ref.shape[1]
    blk = MOBA_BLOCK
    nb = s_len // blk
    scale = HEAD_DIM ** -0.5

    for t in range(2):
        bkt = bkt_ref[t]
        bias = jnp.zeros((blk, blk), F32)
        for r in range(REL_BUCKETS):
            bias = jnp.where(bkt == r, rb_ref[r, h], bias)
        bias_ref[t] = bias
    bias_far = rb_ref[REL_BUCKETS - 1, h]

    row = lax.broadcasted_iota(I32, (blk, blk), 0)
    col = lax.broadcasted_iota(I32, (blk, blk), 1)
    causal = col <= row

    k_means = [jnp.mean(k_ref[0, n * blk:(n + 1) * blk, :], axis=0, keepdims=True) for n in range(nb)]

    for qb in range(nb):
        q = q_ref[0, qb * blk:(qb + 1) * blk, :]
        qs = (q * scale).astype(BF16)
        if qb > MOBA_TOPK:
            gates = [jnp.sum(q * k_means[n], axis=-1, keepdims=True) for n in range(qb)]
            sels = _topk_select(gates, MOBA_TOPK)
        else:
            sels = [None] * qb
        s = _dot_nt(qs, k_ref[0, qb * blk:(qb + 1) * blk, :]) + bias_ref[0]
        s = jnp.where(causal, s, NEG_INF)
        m = jnp.max(s, axis=-1, keepdims=True)
        p = jnp.exp(s - m)
        l = jnp.sum(p, axis=-1, keepdims=True)
        acc = _dot(p, v_ref[0, qb * blk:(qb + 1) * blk, :])
        for n in range(qb - 1, -1, -1):
            s = _dot_nt(qs, k_ref[0, n * blk:(n + 1) * blk, :])
            s = s + (bias_ref[1] if n == qb - 1 else bias_far)
            if sels[n] is not None:
                s = jnp.where(sels[n], s, NEG_INF)
            m, l, acc = _softmax_step(s, v_ref[0, n * blk:(n + 1) * blk, :], m, l, acc)
        o_ref[0, qb * blk:(qb + 1) * blk, :] = acc / l


def moba_prompt(q, k, v, rel_bias):
    b, s_len, width = q.shape
    assert s_len % MOBA_BLOCK == 0 and width == A_HEADS * HEAD_DIM
    ar = np.arange(MOBA_BLOCK)
    d_loc = ar[:, None] - ar[None, :]
    bkt = np.stack([_rel_bucket_np(d_loc), _rel_bucket_np(d_loc + MOBA_BLOCK)]).astype(np.int32)
    assert int(_rel_bucket_np(np.array([MOBA_BLOCK + 1]))[0]) == REL_BUCKETS - 1
    spec = pl.BlockSpec((1, s_len, HEAD_DIM), lambda i, h: (i, 0, h))
    return pl.pallas_call(
        _moba_prompt_kernel,
        out_shape=jax.ShapeDtypeStruct((b, s_len, width), F32),
        grid=(b, A_HEADS),
        in_specs=[pl.BlockSpec(memory_space=pltpu.SMEM),
                  pl.BlockSpec((2, MOBA_BLOCK, MOBA_BLOCK), lambda i, h: (0, 0, 0)),
                  spec, spec, spec],
        out_specs=spec,
        scratch_shapes=[pltpu.VMEM((2, MOBA_BLOCK, MOBA_BLOCK), F32)],
        compiler_params=_cparams(("arbitrary", "arbitrary")),
        name="moba_prompt",
    )(rel_bias, jnp.asarray(bkt), q, k, v)


def _moba_sample_kernel(pt_ref, rb_ref, bkt_ref, q_ref, kn_ref, vn_ref, *rest, past_len):
    nh = A_HEADS
    k_pages = rest[0:2]
    v_pages = rest[2:4]
    o_ref, bias_ref, m_ref, l_ref, g_ref, acc_ref = rest[4:]
    page = k_pages[0].shape[1] // nh

    def head_rows(ref, h):
        return ref[0, pl.ds(h, page, stride=nh), :]

    n = pl.program_id(1)
    nb = pl.num_programs(1)
    t = q_ref.shape[1]
    dh = HEAD_DIM
    blk = MOBA_BLOCK
    scale = dh ** -0.5
    n_last = past_len // blk - 1

    @pl.when(n == 0)
    def _():
        for h in range(nh):
            bkt = bkt_ref[...]
            bias = jnp.zeros((t, blk), F32)
            for r in range(REL_BUCKETS):
                bias = jnp.where(bkt == r, rb_ref[r, h], bias)
            bias_ref[h * t:(h + 1) * t, :] = bias

    q_all = q_ref[0]
    qhs = [q_all[:, h * dh:(h + 1) * dh] for h in range(nh)]
    qss = [(qh * scale).astype(BF16) for qh in qhs]
    heads = range(nh)
    k_blks = [jnp.concatenate([head_rows(k_pages[0], h), head_rows(k_pages[1], h)], axis=0) for h in heads]
    v_blks = [jnp.concatenate([head_rows(v_pages[0], h), head_rows(v_pages[1], h)], axis=0) for h in heads]
    ss = [_dot_nt(qss[h], k_blks[h]) for h in heads]
    gs = [jnp.sum(qhs[h] * jnp.mean(k_blks[h], axis=0, keepdims=True), axis=-1, keepdims=True) for h in heads]
    ss = [ss[h] + jnp.where(n == n_last, bias_ref[h * t:(h + 1) * t, :], rb_ref[REL_BUCKETS - 1, h])
          for h in heads]
    ms = [jnp.max(s, axis=-1, keepdims=True) for s in ss]
    ps = [jnp.exp(s - m) for s, m in zip(ss, ms)]
    accs = [_dot(ps[h], v_blks[h]) for h in heads]
    m_ref[n] = jnp.concatenate(ms, axis=0)
    l_ref[n] = jnp.concatenate([jnp.sum(p, axis=-1, keepdims=True) for p in ps], axis=0)
    g_ref[n] = jnp.concatenate(gs, axis=0)
    acc_ref[n] = jnp.concatenate(accs, axis=0)

    @pl.when(n == nb - 1)
    def _():
        n_blocks = past_len // blk
        rows = nh * t
        sels = _topk_select([g_ref[i] for i in range(n_blocks)], MOBA_TOPK)
        tq = lax.broadcasted_iota(I32, (t, t), 0)
        tk = lax.broadcasted_iota(I32, (t, t), 1)
        s_locs = []
        for h in range(nh):
            bias_loc = jnp.zeros((t, t), F32)
            for d in range(t):
                bias_loc = jnp.where(tq - tk == d, rb_ref[d, h], bias_loc)
            s_loc = _dot_nt(qss[h], kn_ref[0, :, h * dh:(h + 1) * dh]) + bias_loc
            s_locs.append(jnp.where(tk <= tq, s_loc, NEG_INF))
        s_loc = jnp.concatenate(s_locs, axis=0)
        m_tot = jnp.max(s_loc, axis=-1, keepdims=True)
        for i in range(n_blocks):
            m_tot = jnp.maximum(m_tot, jnp.where(sels[i], m_ref[i], NEG_INF))
        p_loc = jnp.exp(s_loc - m_tot)
        l_tot = jnp.sum(p_loc, axis=-1, keepdims=True)
        acc = jnp.concatenate([_dot(p_loc[h * t:(h + 1) * t, :], vn_ref[0, :, h * dh:(h + 1) * dh])
                               for h in range(nh)], axis=0)
        for i in range(n_blocks):
            w = jnp.where(sels[i], jnp.exp(jnp.where(sels[i], m_ref[i] - m_tot, 0.0)), 0.0)
            l_tot = l_tot + w * l_ref[i]
            acc = acc + w * acc_ref[i]
        out = acc / l_tot
        o_ref[0] = jnp.concatenate([out[h * t:(h + 1) * t, :] for h in range(nh)], axis=1)


def moba_sample(q, k_new, v_new, cache_k, cache_v, layer, page_table, rel_bias, past_len):
    db, t, width = q.shape
    page = cache_k.shape[2]
    assert MOBA_BLOCK == 2 * page and past_len % MOBA_BLOCK == 0 and t <= REL_BUCKETS // 2
    assert past_len // MOBA_BLOCK >= MOBA_TOPK and t % SUBLANES == 0
    n_blocks = past_len // MOBA_BLOCK
    rows = A_HEADS * t
    d_last = MOBA_BLOCK + np.arange(t)[:, None] - np.arange(MOBA_BLOCK)[None, :]
    bkt = _rel_bucket_np(d_last).astype(np.int32)
    assert int(_rel_bucket_np(np.array([MOBA_BLOCK + 1]))[0]) == REL_BUCKETS - 1
    tok_spec = pl.BlockSpec((1, t, width), lambda i, n, pt: (i, 0, 0))

    n_phys = cache_k.shape[1]
    ck = cache_k.reshape(cache_k.shape[0] * n_phys, page * A_HEADS, HEAD_DIM)
    cv = cache_v.reshape(cache_v.shape[0] * n_phys, page * A_HEADS, HEAD_DIM)

    def page_spec(j):
        return pl.BlockSpec((1, page * A_HEADS, HEAD_DIM),
                            lambda i, n, pt: (layer * n_phys + pt[i, 2 * n + j], 0, 0))

    grid_spec = pltpu.PrefetchScalarGridSpec(
        num_scalar_prefetch=1,
        grid=(db, n_blocks),
        in_specs=[pl.BlockSpec(memory_space=pltpu.SMEM),
                  pl.BlockSpec((t, MOBA_BLOCK), lambda i, n, pt: (0, 0)),
                  tok_spec, tok_spec, tok_spec, page_spec(0), page_spec(1), page_spec(0), page_spec(1)],
        out_specs=tok_spec,
        scratch_shapes=[pltpu.VMEM((rows, MOBA_BLOCK), F32),
                        pltpu.VMEM((n_blocks, rows, 1), F32),
                        pltpu.VMEM((n_blocks, rows, 1), F32),
                        pltpu.VMEM((n_blocks, rows, 1), F32),
                        pltpu.VMEM((n_blocks, rows, HEAD_DIM), F32)],
    )
    return pl.pallas_call(
        functools.partial(_moba_sample_kernel, past_len=past_len),
        out_shape=jax.ShapeDtypeStruct((db, t, width), F32),
        grid_spec=grid_spec,
        compiler_params=_cparams(("arbitrary", "arbitrary")),
        name="moba_sample",
    )(page_table, rel_bias, jnp.asarray(bkt), q, k_new, v_new, ck, ck, cv, cv)


INV_BASE = 16


def _unit_lower_inverses(a_mats, ri, ci, size):
    base = min(INV_BASE, size)
    same = ri // base == ci // base
    eye = jnp.where(ri == ci, 1.0, 0.0)
    pws = [jnp.where(same, a, 0.0) for a in a_mats]
    invs = [eye - p for p in pws]
    for _ in range(max(int(math.log2(base)) - 1, 0)):
        pws = [_mm(p, p, "nn", 1) for p in pws]
        invs = [i + _mm(i, p, "nn", 1) for i, p in zip(invs, pws)]
    blk = base
    while blk < size:
        pr = ri // blk
        pc = ci // blk
        join = (pr == pc + 1) & (pr // 2 == pc // 2)
        halves = [_mm(i, jnp.where(join, a, 0.0), "nn", 1) for i, a in zip(invs, a_mats)]
        invs = [i - _mm(hf, i, "nn", 1) for i, hf in zip(invs, halves)]
        blk *= 2
    return invs


def _delta_kernel(x_ref, cw_ref, z_ref, sm_ref, smt_ref, ar_ref, dr_ref, ac_ref, dc_ref, nw_ref, s0_ref, b0_ref,
                  y_ref, sn_ref, tail_ref, s_sc, carry_sc):
    c = pl.program_id(1)
    nc = pl.num_programs(1)
    chunk = x_ref.shape[1]
    dk = HEAD_DIM
    hh = G_HEADS
    part = hh * dk

    @pl.when(c == 0)
    def _():
        s_sc[...] = s0_ref[0]
        carry_sc[...] = b0_ref[0]

    x = x_ref[0]
    xx = jnp.concatenate([carry_sc[...], x], axis=0)
    cw = cw_ref[...]
    y = x * cw[CONV_W - 1:CONV_W, :]
    for s in range(1, CONV_W):
        y = y + xx[SUBLANES - s:SUBLANES - s + chunk, :] * cw[CONV_W - 1 - s:CONV_W - s, :]
    carry_sc[...] = xx[chunk:chunk + SUBLANES, :]
    qkv = _silu(y)

    def softplus(t):
        return jnp.maximum(t, 0.0) + jnp.log(1.0 + jnp.exp(-jnp.abs(t)))

    sm = sm_ref[0]
    beta_all = jax.nn.sigmoid(sm)
    g_cols = -jnp.exp(ar_ref[...]) * softplus(sm + dr_ref[...])
    g_rows = -jnp.exp(ac_ref[...]) * softplus(smt_ref[0] + dc_ref[...])
    ri = lax.broadcasted_iota(I32, (chunk, chunk), 0)
    ci = lax.broadcasted_iota(I32, (chunk, chunk), 1)
    incl = ci <= ri
    strict = ci < ri
    gc_cols = _dot_exact(jnp.where(incl, 1.0, 0.0), g_cols)
    gc_rows = _dot_exact(g_rows, jnp.where(ri <= ci, 1.0, 0.0))
    nw = nw_ref[...]

    heads = range(hh)
    qs, ks, vs, betas, gcs, dmasks = [], [], [], [], [], []
    for h in heads:
        q = qkv[:, h * dk:(h + 1) * dk]
        k = qkv[:, part + h * dk:part + (h + 1) * dk]
        qs.append(q * lax.rsqrt(jnp.sum(q * q, axis=-1, keepdims=True) + EPS) * (dk ** -0.5))
        ks.append(k * lax.rsqrt(jnp.sum(k * k, axis=-1, keepdims=True) + EPS))
        vs.append(qkv[:, 2 * part + h * dk:2 * part + (h + 1) * dk])
        betas.append(beta_all[:, h:h + 1])
        gc_c = gc_cols[:, hh + h:hh + h + 1]
        gc_r = gc_rows[hh + h:hh + h + 1, :]
        gcs.append(gc_c)
        dmasks.append(jnp.exp(jnp.where(incl, gc_c - gc_r, NEG_INF)))

    kbs = [k * b for k, b in zip(ks, betas)]
    a_mats = [jnp.where(strict, _mm(kb, k, "nt", 1) * dm, 0.0) for kb, k, dm in zip(kbs, ks, dmasks)]
    t_invs = _unit_lower_inverses(a_mats, ri, ci, chunk)
    egs = [jnp.exp(g) for g in gcs]
    uws = [_mm(t, jnp.concatenate([v * b, kb * eg], axis=1), "nn", 1)
           for t, v, b, kb, eg in zip(t_invs, vs, betas, kbs, egs)]
    s_mats = [s_sc[h] for h in heads]
    v_news = [uw[:, :dk] - _mm(uw[:, dk:], s, "nn", 1) for uw, s in zip(uws, s_mats)]
    attns = [_mm(q, k, "nt", 1) * dm for q, k, dm in zip(qs, ks, dmasks)]
    outs = [_mm(q * eg, s, "nn", 1) + _mm(at, vn, "nn", 1)
            for q, eg, s, at, vn in zip(qs, egs, s_mats, attns, v_news)]
    for h in heads:
        g_last = gcs[h][chunk - 1:chunk, :]
        s_sc[h] = s_mats[h] * jnp.exp(g_last) + _mm(ks[h] * jnp.exp(g_last - gcs[h]), v_news[h], "tn", 1)
    for h in heads:
        o = outs[h]
        o_n = o * lax.rsqrt(jnp.mean(o * o, axis=-1, keepdims=True) + EPS) * nw
        y_ref[0, :, h * dk:(h + 1) * dk] = o_n * _silu(z_ref[0, :, h * dk:(h + 1) * dk])

    @pl.when(c == nc - 1)
    def _():
        sn_ref[0] = s_sc[...]
        tail_ref[0] = carry_sc[...]


def delta_rule(xg, z, small, small_t, conv_w, a_log, dt_bias, delta_norm, s0, buf0, chunk):
    b, l, cw_ch = xg.shape
    hh = G_HEADS
    assert l % chunk == 0 and cw_ch == 3 * hh * HEAD_DIM and l >= CONV_W - 1
    nc = l // chunk
    width = hh * HEAD_DIM
    a_row = jnp.pad(a_log.reshape(1, hh), ((0, 0), (hh, LANES - 2 * hh)))
    d_row = jnp.pad(dt_bias.reshape(1, hh), ((0, 0), (hh, LANES - 2 * hh)))
    a_col = jnp.pad(a_log.reshape(hh, 1), ((hh, 0), (0, 0)))
    d_col = jnp.pad(dt_bias.reshape(hh, 1), ((hh, 0), (0, 0)))

    def const(shape):
        return pl.BlockSpec(shape, lambda i, c: (0,) * len(shape))

    return pl.pallas_call(
        _delta_kernel,
        out_shape=[jax.ShapeDtypeStruct((b, l, width), F32),
                   jax.ShapeDtypeStruct((b, hh, HEAD_DIM, HEAD_DIM), F32),
                   jax.ShapeDtypeStruct((b, SUBLANES, cw_ch), F32)],
        grid=(b, nc),
        in_specs=[pl.BlockSpec((1, chunk, cw_ch), lambda i, c: (i, c, 0)),
                  const((CONV_W, cw_ch)),
                  pl.BlockSpec((1, chunk, width), lambda i, c: (i, c, 0)),
                  pl.BlockSpec((1, chunk, LANES), lambda i, c: (i, c, 0)),
                  pl.BlockSpec((1, 2 * hh, chunk), lambda i, c: (i, 0, c)),
                  const((1, LANES)), const((1, LANES)), const((2 * hh, 1)), const((2 * hh, 1)),
                  const((1, HEAD_DIM)),
                  pl.BlockSpec((1, hh, HEAD_DIM, HEAD_DIM), lambda i, c: (i, 0, 0, 0)),
                  pl.BlockSpec((1, SUBLANES, cw_ch), lambda i, c: (i, 0, 0))],
        out_specs=[pl.BlockSpec((1, chunk, width), lambda i, c: (i, c, 0)),
                   pl.BlockSpec((1, hh, HEAD_DIM, HEAD_DIM), lambda i, c: (i, 0, 0, 0)),
                   pl.BlockSpec((1, SUBLANES, cw_ch), lambda i, c: (i, 0, 0))],
        scratch_shapes=[pltpu.VMEM((hh, HEAD_DIM, HEAD_DIM), F32),
                        pltpu.VMEM((SUBLANES, cw_ch), F32)],
        compiler_params=_cparams(("arbitrary", "arbitrary")),
        name="delta_rule",
    )(xg, conv_w, z, small, small_t, a_row, d_row, a_col, d_col, delta_norm.reshape(1, HEAD_DIM), s0, buf0)


def _post_kernel(ya_ref, yb_ref, ga_ref, gb_ref, x_ref, gm_ref, scf_ref, shf_ref, wa_ref, wb_ref, wo_ref,
                 nw_ref, rw_ref, x1_ref, h2_ref, lg_ref):
    bt, tl, d = x_ref.shape
    rows = bt * tl
    ya = ya_ref[...].reshape(rows, -1)
    yb = yb_ref[...].reshape(rows, -1)
    merged = (jax.nn.sigmoid(ga_ref[...].reshape(rows, d)) * _dot(ya, wa_ref[...])
              + jax.nn.sigmoid(gb_ref[...].reshape(rows, d)) * _dot(yb, wb_ref[...]))
    mix = _dot(merged, wo_ref[...]).reshape(bt, tl, d)
    x1 = x_ref[...] + gm_ref[:, 0] * mix
    x1_ref[...] = x1
    y = x1 * lax.rsqrt(jnp.mean(x1 * x1, axis=-1, keepdims=True) + EPS) * nw_ref[...]
    h2 = (y * (1.0 + scf_ref[:, 0]) + shf_ref[:, 0]).reshape(rows, d)
    h2_ref[...] = _pack_pairs(h2).reshape(bt, tl, d // 2)
    lg_ref[...] = _mm(h2, rw_ref[...], "nn", 3).reshape(bt, tl, -1)


def post_mix(y_a, y_b, gate_a, gate_b, x, mod4, w_a, w_b, w_o, norm_w, router_w, bt, tl):
    b, l, d = x.shape
    wa_in = y_a.shape[-1]
    wb_in = y_b.shape[-1]
    n_e = router_w.shape[1]

    def tok(width):
        return pl.BlockSpec((bt, tl, width), lambda i, t: (i, t, 0))

    def modspec(idx):
        return pl.BlockSpec((bt, 1, 1, d), lambda i, t, idx=idx: (i, idx, 0, 0))

    def const(shape):
        return pl.BlockSpec(shape, lambda i, t: (0, 0), pipeline_mode=pl.Buffered(1))

    return pl.pallas_call(
        _post_kernel,
        out_shape=[jax.ShapeDtypeStruct((b, l, d), F32), jax.ShapeDtypeStruct((b, l, d // 2), jnp.uint32),
                   jax.ShapeDtypeStruct((b, l, n_e), F32)],
        grid=(b // bt, l // tl),
        in_specs=[tok(wa_in), tok(wb_in), tok(d), tok(d), tok(d), modspec(2), modspec(4), modspec(3),
                  const((wa_in, d)), const((wb_in, d)), const((d, d)), const((1, d)), const((d, n_e))],
        out_specs=[tok(d), tok(d // 2), tok(n_e)],
        compiler_params=_cparams(("arbitrary", "arbitrary")),
        name="post_mix",
    )(y_a, y_b, gate_a, gate_b, x, mod4, mod4, mod4, w_a, w_b, w_o, norm_w.reshape(1, d), router_w)


def _route_kernel(lg_ref, rb_ref, e_ref, w_ref, p_ref, cnt_ref, run_sc, *, n_e):
    i = pl.program_id(0)
    tm, width = lg_ref.shape
    per_group = n_e // N_GROUPS

    @pl.when(i == 0)
    def _():
        run_sc[...] = jnp.zeros_like(run_sc)

    lane = lax.broadcasted_iota(I32, (tm, width), 1)
    scores = jnp.where(lane < n_e, jax.nn.sigmoid(lg_ref[...]), 0.0)
    biased = jnp.where(lane < n_e, scores + rb_ref[...], NEG_INF)
    grp = lane // per_group

    def first_argmax(vals):
        mx = jnp.max(vals, axis=-1, keepdims=True)
        idx = jnp.min(jnp.where(vals == mx, lane, width), axis=-1, keepdims=True)
        return mx, idx

    group_scores = []
    for g in range(N_GROUPS):
        mg = jnp.where(grp == g, biased, NEG_INF)
        m1, i1 = first_argmax(mg)
        m2 = jnp.max(jnp.where(lane == i1, NEG_INF, mg), axis=-1, keepdims=True)
        group_scores.append(m1 + m2)
    keep = _topk_select(group_scores, TOPK_GROUPS)
    expert_keep = jnp.zeros((tm, width), jnp.bool_)
    for g in range(N_GROUPS):
        expert_keep = expert_keep | ((grp == g) & keep[g])
    masked = jnp.where(expert_keep, biased, NEG_INF)

    sel = jnp.zeros((tm, width), jnp.bool_)
    idxs = []
    for _ in range(TOP_K):
        _, idx = first_argmax(masked)
        hit = lane == idx
        sel = sel | hit
        masked = jnp.where(hit, NEG_INF, masked)
        idxs.append(idx)
    sel_f = jnp.where(sel, 1.0, 0.0)
    top_w = scores * sel_f
    top_w = top_w / jnp.sum(top_w, axis=-1, keepdims=True) * ROUTED_SCALE

    ri = lax.broadcasted_iota(I32, (tm, tm), 0)
    ci = lax.broadcasted_iota(I32, (tm, tm), 1)
    before = _dot(jnp.where(ci < ri, 1.0, 0.0), sel_f) + run_sc[...]
    run_sc[...] = run_sc[...] + jnp.sum(sel_f, axis=0, keepdims=True)
    cnt_ref[...] = run_sc[...].astype(I32)

    lane_o = lax.broadcasted_iota(I32, (tm, LANES), 1)
    e_out = jnp.zeros((tm, LANES), I32)
    w_out = jnp.zeros((tm, LANES), F32)
    p_out = jnp.zeros((tm, LANES), I32)
    for kk, idx in enumerate(idxs):
        hit = lane == idx
        wk = jnp.sum(jnp.where(hit, top_w, 0.0), axis=-1, keepdims=True)
        pk = jnp.sum(jnp.where(hit, before, 0.0), axis=-1, keepdims=True).astype(I32)
        e_out = jnp.where(lane_o == kk, idx, e_out)
        w_out = jnp.where(lane_o == kk, wk, w_out)
        p_out = jnp.where(lane_o == kk, pk, p_out)
    e_ref[...] = e_out
    w_ref[...] = w_out
    p_ref[...] = p_out


def route(logits, router_bias, n_e, tm):
    n, width = logits.shape
    assert n % tm == 0 and width == LANES and n_e <= LANES
    tok = pl.BlockSpec((tm, LANES), lambda i: (i, 0))
    one = pl.BlockSpec((1, LANES), lambda i: (0, 0))
    bias = jnp.pad(router_bias.reshape(1, n_e), ((0, 0), (0, LANES - n_e)))
    return pl.pallas_call(
        functools.partial(_route_kernel, n_e=n_e),
        out_shape=[jax.ShapeDtypeStruct((n, LANES), I32), jax.ShapeDtypeStruct((n, LANES), F32),
                   jax.ShapeDtypeStruct((n, LANES), I32), jax.ShapeDtypeStruct((1, LANES), I32)],
        grid=(n // tm,),
        in_specs=[tok, one],
        out_specs=[tok, tok, tok, one],
        scratch_shapes=[pltpu.VMEM((1, LANES), F32)],
        compiler_params=_cparams(("arbitrary",)),
        name="route",
    )(logits, bias)


def _row_copy(src_ref, dst_ref, src_row, dst_row, sem):
    return pltpu.make_async_copy(src_ref.at[pl.ds(src_row, 1)], dst_ref.at[pl.ds(dst_row, 1)], sem)


def _dispatch_kernel(dest_ref, h_ref, hs_in_ref, hs_ref, sem):
    del hs_in_ref
    tm = h_ref.shape[0]

    def body(t, carry):
        for kk in range(TOP_K):
            _row_copy(h_ref, hs_ref, t, dest_ref[0, 0, t * TOP_K + kk], sem).start()
        return carry

    def drain(t, carry):
        for kk in range(TOP_K):
            _row_copy(h_ref, hs_ref, t, dest_ref[0, 0, t * TOP_K + kk], sem).wait()
        return carry

    lax.fori_loop(0, tm, body, 0)
    lax.fori_loop(0, tm, drain, 0)


def dispatch(h_all, dest, n_rows, tm):
    n, d = h_all.shape
    assert n % tm == 0
    dest3 = dest.reshape(n // tm, 1, tm * TOP_K)
    zeros = jnp.zeros((n_rows, d), h_all.dtype)
    return pl.pallas_call(
        _dispatch_kernel,
        out_shape=jax.ShapeDtypeStruct((n_rows, d), h_all.dtype),
        grid=(n // tm,),
        in_specs=[pl.BlockSpec((1, 1, tm * TOP_K), lambda i: (i, 0, 0), memory_space=pltpu.SMEM),
                  pl.BlockSpec((tm, d), lambda i: (i, 0)),
                  pl.BlockSpec(memory_space=pl.ANY)],
        out_specs=pl.BlockSpec(memory_space=pl.ANY),
        scratch_shapes=[pltpu.SemaphoreType.DMA(())],
        input_output_aliases={2: 0},
        compiler_params=_cparams(("arbitrary",)),
        name="dispatch",
    )(dest3, h_all, zeros)


def _experts_kernel(be_ref, bf_ref, nu_ref, x_ref, wg_ref, wu_ref, wd_ref, y_ref, wg_sc, wu_sc, wd_sc):
    i = pl.program_id(0)

    @pl.when(bf_ref[i] == 1)
    def _():
        wg_sc[...] = wg_ref[0].astype(BF16)
        wu_sc[...] = wu_ref[0].astype(BF16)
        wd_sc[...] = wd_ref[0].astype(BF16)

    @pl.when(i < nu_ref[0])
    def _():
        y_ref[...] = _pack_pairs(_swiglu_packed(x_ref[...], wg_sc, wu_sc, wd_sc))

    @pl.when(i >= nu_ref[0])
    def _():
        y_ref[...] = jnp.zeros_like(y_ref)


def _swiglu_packed(xp, wg_ref, wu_ref, wd_ref):
    lo, hi = _unpack_pairs(xp)
    lo = lo.astype(BF16)
    hi = hi.astype(BF16)
    half = xp.shape[-1]

    def proj(w_ref):
        return (jnp.dot(lo, w_ref[:half, :], preferred_element_type=F32)
                + jnp.dot(hi, w_ref[half:, :], preferred_element_type=F32))

    act = (_silu(proj(wg_ref)) * proj(wu_ref)).astype(BF16)
    return jnp.dot(act, wd_ref[...], preferred_element_type=F32)


def experts(h_sorted, blk_e, blk_first, n_used, w_gate, w_up, w_down):
    n_rows, dp = h_sorted.shape
    n_blocks = n_rows // EXPERT_ROWS
    d, de = w_gate.shape[-2:]
    assert d == 2 * dp
    grid_spec = pltpu.PrefetchScalarGridSpec(
        num_scalar_prefetch=3,
        grid=(n_blocks,),
        in_specs=[pl.BlockSpec((EXPERT_ROWS, dp), lambda i, be, bf, nu: (i, 0)),
                  pl.BlockSpec((1, d, de), lambda i, be, bf, nu: (be[i], 0, 0)),
                  pl.BlockSpec((1, d, de), lambda i, be, bf, nu: (be[i], 0, 0)),
                  pl.BlockSpec((1, de, d), lambda i, be, bf, nu: (be[i], 0, 0))],
        out_specs=pl.BlockSpec((EXPERT_ROWS, dp), lambda i, be, bf, nu: (i, 0)),
        scratch_shapes=[pltpu.VMEM((d, de), BF16), pltpu.VMEM((d, de), BF16), pltpu.VMEM((de, d), BF16)],
    )
    return pl.pallas_call(
        _experts_kernel,
        out_shape=jax.ShapeDtypeStruct((n_rows, dp), jnp.uint32),
        grid_spec=grid_spec,
        compiler_params=_cparams(("arbitrary",)),
        name="experts",
    )(blk_e, blk_first, n_used, h_sorted, w_gate, w_up, w_down)


def _combine_kernel(dest_ref, x1_ref, h2_ref, gf_ref, w_ref, wg_ref, wu_ref, wd_ref, nw_ref, yr_ref, o_ref,
                    gbuf, sem):
    bt, tl, d = x1_ref.shape
    tm = bt * tl

    def body(t, carry):
        for kk in range(TOP_K):
            _row_copy(yr_ref, gbuf.at[kk], dest_ref[0, 0, t * TOP_K + kk], t, sem).start()
        return carry

    lax.fori_loop(0, tm, body, 0)
    y = _swiglu_packed(h2_ref[...].reshape(tm, d // 2), wg_ref, wu_ref, wd_ref)
    wts = w_ref[...]

    def drain(t, carry):
        for kk in range(TOP_K):
            _row_copy(yr_ref, gbuf.at[kk], dest_ref[0, 0, t * TOP_K + kk], t, sem).wait()
        return carry

    lax.fori_loop(0, tm, drain, 0)
    r_lo = jnp.zeros((tm, d // 2), F32)
    r_hi = jnp.zeros((tm, d // 2), F32)
    for kk in range(TOP_K):
        lo, hi = _unpack_pairs(gbuf[kk])
        r_lo = r_lo + lo * wts[:, kk:kk + 1]
        r_hi = r_hi + hi * wts[:, kk:kk + 1]
    y = y + jnp.concatenate([r_lo, r_hi], axis=1)
    x2 = x1_ref[...] + gf_ref[:, 0] * y.reshape(bt, tl, d)
    o_ref[...] = x2 * lax.rsqrt(jnp.mean(x2 * x2, axis=-1, keepdims=True) + EPS) * nw_ref[...]


def combine(x1, h2, mod4, sel_w, dest, y_rows, w_gs, w_us, w_ds, norm_final, bt, tl):
    b, l, d = x1.shape
    tm = bt * tl
    n = b * l
    ds = w_gs.shape[1]
    n_t = l // tl
    dest3 = dest.reshape(n // tm, 1, tm * TOP_K)

    def const(shape):
        return pl.BlockSpec(shape, lambda i, t: (0, 0), pipeline_mode=pl.Buffered(1))

    return pl.pallas_call(
        _combine_kernel,
        out_shape=jax.ShapeDtypeStruct((b, l, d), F32),
        grid=(b // bt, n_t),
        in_specs=[pl.BlockSpec((1, 1, tm * TOP_K), lambda i, t: (i * n_t + t, 0, 0), memory_space=pltpu.SMEM),
                  pl.BlockSpec((bt, tl, d), lambda i, t: (i, t, 0)),
                  pl.BlockSpec((bt, tl, d // 2), lambda i, t: (i, t, 0)),
                  pl.BlockSpec((bt, 1, 1, d), lambda i, t: (i, 5, 0, 0)),
                  pl.BlockSpec((tm, LANES), lambda i, t: (i * n_t + t, 0)),
                  const((d, ds)), const((d, ds)), const((ds, d)), const((1, d)),
                  pl.BlockSpec(memory_space=pl.ANY)],
        out_specs=pl.BlockSpec((bt, tl, d), lambda i, t: (i, t, 0)),
        scratch_shapes=[pltpu.VMEM((TOP_K, tm, d // 2), jnp.uint32), pltpu.SemaphoreType.DMA(())],
        compiler_params=_cparams(("arbitrary", "arbitrary")),
        name="combine",
    )(dest3, x1, h2, mod4, sel_w, w_gs, w_us, w_ds, norm_final.reshape(1, d), y_rows)


def _moe(h2_p, h2_s, x1_p, x1_s, lg_p, lg_s, mod_p, mod_s, router_bias, w_gate_e, w_up_e, w_down_e,
         w_gs, w_us, w_ds, norm_final, tiles_p, tiles_s):
    d = h2_p.shape[-1]
    n_p = h2_p.shape[0] * h2_p.shape[1]
    n_s = h2_s.shape[0] * h2_s.shape[1]
    n = n_p + n_s
    n_e = w_gate_e.shape[0]
    h_all = jnp.concatenate([h2_p.reshape(n_p, d), h2_s.reshape(n_s, d)], axis=0)
    logits = jnp.concatenate([lg_p.reshape(n_p, LANES), lg_s.reshape(n_s, LANES)], axis=0)
    sel_e, sel_w, sel_pos, counts = route(logits, router_bias, n_e, 512)
    counts = counts[0, :n_e]
    padded = (counts + EXPERT_ROWS - 1) // EXPERT_ROWS * EXPERT_ROWS
    pad_ends = jnp.cumsum(padded)
    pad_starts = pad_ends - padded
    n_blocks = -(-(n * TOP_K + n_e * (EXPERT_ROWS - 1)) // EXPERT_ROWS)
    n_rows = n_blocks * EXPERT_ROWS
    dest = pad_starts[sel_e[:, :TOP_K]] + sel_pos[:, :TOP_K]
    blk_start = jnp.arange(n_blocks, dtype=I32) * EXPERT_ROWS
    n_used = (pad_ends[-1] // EXPERT_ROWS).astype(I32)
    blk_e = jnp.minimum(jnp.sum((blk_start[:, None] >= pad_ends[None, :]).astype(I32), axis=1), n_e - 1)
    last_used_e = blk_e[jnp.maximum(n_used - 1, 0)]
    blk_e = jnp.where(jnp.arange(n_blocks) < n_used, blk_e, last_used_e)
    blk_first = jnp.concatenate([jnp.ones((1,), I32), (blk_e[1:] != blk_e[:-1]).astype(I32)])
    h_sorted = dispatch(h_all, dest, n_rows, 512)
    y_rows = experts(h_sorted, blk_e, blk_first, n_used.reshape(1), w_gate_e, w_up_e, w_down_e)
    y_p = combine(x1_p, h2_p, mod_p, sel_w[:n_p], dest[:n_p], y_rows, w_gs, w_us, w_ds, norm_final, *tiles_p)
    y_s = combine(x1_s, h2_s, mod_s, sel_w[n_p:], dest[n_p:], y_rows, w_gs, w_us, w_ds, norm_final, *tiles_s)
    return y_p, y_s


def kernel(x_prompt, x_sample, cache_k, cache_v, state_delta, state_conv, page_table, c_prompt, c_sample,
           w_ada, b_ada, norm_mix, w_in, conv_w, a_log, dt_bias, delta_norm, w_branch_a, w_branch_b, w_out,
           rel_bias, norm_ffn, router_w, router_bias, w_gate_e, w_up_e, w_down_e, w_gate_s, w_up_s, w_down_s,
           norm_final):
    depth = w_ada.shape[0]
    assert depth == 1, "the final norm is fused into the layer's last stage"
    n_b, seq, d = x_prompt.shape
    d_b, d_seq, _ = x_sample.shape
    n_pages = page_table.shape[1]
    page = cache_k.shape[2]
    past_len = n_pages * page
    a_width = A_HEADS * HEAD_DIM
    g_width = G_HEADS * HEAD_DIM
    conv_ch = 3 * g_width
    l = 0

    off_small = 3 * a_width + conv_ch + g_width
    w_l = w_in[l]
    w_main = jnp.concatenate([w_l[:, :off_small], w_l[:, off_small + 2 * G_HEADS:]], axis=1).astype(BF16)
    w_small = jnp.pad(w_l[:, off_small:off_small + 2 * G_HEADS], ((0, 0), (0, LANES - 2 * G_HEADS))).astype(BF16)
    assert w_main.shape[1] == IN_COL_TILES * IN_TILE

    c_all = jnp.concatenate([c_prompt, c_sample], axis=0)
    mod = ada_mod(c_all, w_ada[l], b_ada[l])
    mod_p = mod[:n_b].reshape(n_b, N_ADA, 1, d)
    mod_s = mod[n_b:].reshape(d_b, N_ADA, 1, d)

    tiles_p = (1, 512)
    tiles_s = (512 // d_seq, d_seq)
    post_p = (1, 256)
    post_s = (256 // d_seq, d_seq)
    wa = w_branch_a[l].astype(BF16)
    wb = w_branch_b[l].astype(BF16)
    wo = w_out[l].astype(BF16)
    rw = jnp.pad(router_w[l], ((0, 0), (0, LANES - router_w.shape[-1])))

    def mixer(x, mod4, tiles, post_tiles, attend, s0, buf0, chunk):
        a_q, a_k, a_v, g_qkv, g_z, gate_a, gate_b, small = in_proj(x, mod4, norm_mix[l], w_main, w_small, *tiles)
        y_a = attend(a_q, a_k, a_v)
        small_t = jnp.swapaxes(small[..., :2 * G_HEADS], 1, 2)
        y_b, s_new, tail = delta_rule(g_qkv, g_z, small, small_t, conv_w[l], a_log[l], dt_bias[l],
                                      delta_norm[l], s0, buf0, chunk)
        x1, h2, lg = post_mix(y_a, y_b, gate_a, gate_b, x, mod4, wa, wb, wo, norm_ffn[l], rw, *post_tiles)
        return x1, h2, lg, a_k, a_v, s_new, tail[:, SUBLANES - (CONV_W - 1):]

    s0_p = jnp.zeros((n_b, G_HEADS, HEAD_DIM, HEAD_DIM), F32)
    buf0_p = jnp.zeros((n_b, SUBLANES, conv_ch), F32)
    buf0_s = jnp.pad(state_conv[l], ((0, 0), (SUBLANES - (CONV_W - 1), 0), (0, 0)))

    x1_p, h2_p, lg_p, k_p, v_p, d_p, cv_p = mixer(
        x_prompt, mod_p, tiles_p, post_p, functools.partial(moba_prompt, rel_bias=rel_bias), s0_p, buf0_p,
        DELTA_CHUNK)
    x1_s, h2_s, lg_s, k_s, v_s, d_s, cv_s = mixer(
        x_sample, mod_s, tiles_s, post_s,
        functools.partial(moba_sample, cache_k=cache_k, cache_v=cache_v, layer=l, page_table=page_table,
                          rel_bias=rel_bias, past_len=past_len),
        state_delta[l], buf0_s, d_seq)

    comb_p = (1, 256)
    comb_s = (256 // d_seq, d_seq)
    y_p, y_s = _moe(h2_p, h2_s, x1_p, x1_s, lg_p, lg_s, mod_p, mod_s, router_bias[l],
                    w_gate_e[l], w_up_e[l], w_down_e[l],
                    w_gate_s[l].astype(BF16), w_up_s[l].astype(BF16), w_down_s[l].astype(BF16),
                    norm_final, comb_p, comb_s)

    def heads(t, n_heads):
        return t.reshape(t.shape[:-1] + (n_heads, HEAD_DIM))[None]

    return (y_p, y_s, heads(k_p, A_HEADS), heads(v_p, A_HEADS), heads(k_s, A_HEADS), heads(v_s, A_HEADS),
            d_p[None], cv_p[None], d_s[None], cv_s[None])
```

```python
import functools
import math

import numpy as np
import jax
import jax.numpy as jnp
from jax import lax
from jax.experimental import pallas as pl
from jax.experimental.pallas import tpu as pltpu

F32 = jnp.float32
BF16 = jnp.bfloat16
I32 = jnp.int32
EPS = 1e-6
NEG_INF = float("-inf")

A_HEADS = 8
G_HEADS = 8
MOBA_BLOCK = 256
MOBA_TOPK = 3
REL_BUCKETS = 32
REL_MAX_DIST = 128
CONV_W = 4
N_GROUPS = 8
TOPK_GROUPS = 4
TOP_K = 6
ROUTED_SCALE = 2.5
N_ADA = 6
HEAD_DIM = 128

LANES = 128
SUBLANES = 8
VMEM_LIMIT = 56 * 1024 * 1024
EXPERT_ROWS = 256
DELTA_CHUNK = 128
SAMPLE_BLOCKS = 4


def _cparams(sem, vmem=VMEM_LIMIT):
    return pltpu.CompilerParams(dimension_semantics=sem, vmem_limit_bytes=vmem)


def _dot(a, b):
    return jnp.dot(a.astype(BF16), b.astype(BF16), preferred_element_type=F32)


def _dot_nt(a, b):
    return lax.dot_general(a.astype(BF16), b.astype(BF16), (((1,), (1,)), ((), ())),
                           preferred_element_type=F32)


def _dot_tn(a, b):
    return lax.dot_general(a.astype(BF16), b.astype(BF16), (((0,), (0,)), ((), ())),
                           preferred_element_type=F32)


def _dot_exact(a, b):
    return jnp.dot(a, b, precision=lax.Precision.HIGHEST, preferred_element_type=F32)


_DIMS = {"nn": (((1,), (0,)), ((), ())), "nt": (((1,), (1,)), ((), ())), "tn": (((0,), (0,)), ((), ()))}

def _mm(a, b, form, passes):
    dims = _DIMS[form]
    a_hi = a.astype(BF16)
    b_hi = b.astype(BF16)
    out = lax.dot_general(a_hi, b_hi, dims, preferred_element_type=F32)
    if passes == 3:
        a_lo = (a - a_hi.astype(F32)).astype(BF16)
        b_lo = (b - b_hi.astype(F32)).astype(BF16)
        out = out + (lax.dot_general(a_hi, b_lo, dims, preferred_element_type=F32)
                     + lax.dot_general(a_lo, b_hi, dims, preferred_element_type=F32))
    return out


def _pack_pairs(x):
    w = x.shape[-1] // 2
    lo = lax.bitcast_convert_type(x[:, :w].astype(BF16).astype(F32), jnp.uint32)
    hi = lax.bitcast_convert_type(x[:, w:].astype(BF16).astype(F32), jnp.uint32)
    return (lo >> 16) | (hi & jnp.uint32(0xFFFF0000))


def _unpack_pairs(u):
    lo = lax.bitcast_convert_type(u << 16, F32)
    hi = lax.bitcast_convert_type(u & jnp.uint32(0xFFFF0000), F32)
    return lo, hi


def _silu(x):
    return x * jax.nn.sigmoid(x)


def _rel_bucket_np(dist):
    n = np.maximum(dist, 0)
    max_exact = REL_BUCKETS // 2
    nf = np.maximum(n, 1).astype(np.float32)
    large = max_exact + (np.log(nf / np.float32(max_exact)) / np.float32(math.log(REL_MAX_DIST / max_exact))
                         * np.float32(REL_BUCKETS - max_exact)).astype(np.int32)
    return np.where(n < max_exact, n, np.minimum(large, REL_BUCKETS - 1)).astype(np.int32)


def _ada_kernel(c_ref, w_ref, b_ref, o_ref):
    o_ref[...] = _dot(_silu(c_ref[...]), w_ref[...]) + b_ref[...]


def ada_mod(c_all, w_ada, b_ada):
    rows, d = c_all.shape
    n = w_ada.shape[1]
    tn = 1024
    return pl.pallas_call(
        _ada_kernel,
        out_shape=jax.ShapeDtypeStruct((rows, n), F32),
        grid=(n // tn,),
        in_specs=[pl.BlockSpec((rows, d), lambda j: (0, 0)),
                  pl.BlockSpec((d, tn), lambda j: (0, j)),
                  pl.BlockSpec((1, tn), lambda j: (0, j))],
        out_specs=pl.BlockSpec((rows, tn), lambda j: (0, j)),
        compiler_params=_cparams(("arbitrary",)),
        name="ada_mod",
    )(c_all, w_ada, b_ada.reshape(1, n))


IN_TILE = 1024
IN_GROUPS = (("a_q", 0, 1), ("a_k", 1, 1), ("a_v", 2, 1), ("g_qkv", 3, 3), ("g_z", 6, 1),
             ("gate_a", 7, 2), ("gate_b", 9, 2))
IN_COL_TILES = 11


def _inproj_kernel(x_ref, sc_ref, sh_ref, nw_ref, w_ref, ws_ref, *rest):
    out_refs = rest[:len(IN_GROUPS)]
    small_ref = rest[len(IN_GROUPS)]
    h_ref = rest[len(IN_GROUPS) + 1]
    bt, tl, d = x_ref.shape
    j = pl.program_id(2)

    @pl.when(j == 0)
    def _():
        x = x_ref[...]
        y = x * lax.rsqrt(jnp.mean(x * x, axis=-1, keepdims=True) + EPS) * nw_ref[...]
        h = y * (1.0 + sc_ref[:, 0]) + sh_ref[:, 0]
        h2 = h.reshape(bt * tl, d).astype(BF16)
        h_ref[...] = h2
        small_ref[...] = jnp.dot(h2, ws_ref[...], preferred_element_type=F32).reshape(bt, tl, LANES)

    res = jnp.dot(h_ref[...], w_ref[...], preferred_element_type=F32).reshape(bt, tl, IN_TILE)
    for o_ref, (_, j0, nj) in zip(out_refs, IN_GROUPS):
        @pl.when((j >= j0) & (j < j0 + nj))
        def _(o_ref=o_ref):
            o_ref[...] = res


def in_proj(x, mod4, norm_w, w_main, w_small, bt, tl):
    b, l, d = x.shape
    grid = (b // bt, l // tl, IN_COL_TILES)
    out_shapes, out_specs = [], []
    for _, j0, nj in IN_GROUPS:
        out_shapes.append(jax.ShapeDtypeStruct((b, l, nj * IN_TILE), F32))
        out_specs.append(pl.BlockSpec((bt, tl, IN_TILE),
                                      lambda i, t, j, j0=j0, nj=nj: (i, t, jnp.clip(j - j0, 0, nj - 1))))
    out_shapes.append(jax.ShapeDtypeStruct((b, l, LANES), F32))
    out_specs.append(pl.BlockSpec((bt, tl, LANES), lambda i, t, j: (i, t, 0)))
    return pl.pallas_call(
        _inproj_kernel,
        out_shape=out_shapes,
        grid=grid,
        in_specs=[pl.BlockSpec((bt, tl, d), lambda i, t, j: (i, t, 0)),
                  pl.BlockSpec((bt, 1, 1, d), lambda i, t, j: (i, 1, 0, 0)),
                  pl.BlockSpec((bt, 1, 1, d), lambda i, t, j: (i, 0, 0, 0)),
                  pl.BlockSpec((1, d), lambda i, t, j: (0, 0)),
                  pl.BlockSpec((d, IN_TILE), lambda i, t, j: (0, j)),
                  pl.BlockSpec((d, LANES), lambda i, t, j: (0, 0))],
        out_specs=out_specs,
        scratch_shapes=[pltpu.VMEM((bt * tl, d), BF16)],
        compiler_params=_cparams(("arbitrary", "arbitrary", "arbitrary")),
        name="in_proj",
    )(x, mod4, mod4, norm_w.reshape(1, d), w_main, w_small)


def _softmax_step(s, v_blk, m, l, acc):
    m_new = jnp.maximum(m, jnp.max(s, axis=-1, keepdims=True))
    alpha = jnp.exp(m - m_new)
    p = jnp.exp(s - m_new)
    l = alpha * l + jnp.sum(p, axis=-1, keepdims=True)
    acc = alpha * acc + _dot(p, v_blk)
    return m_new, l, acc


def _topk_select(cols, k):
    sels = []
    for n, gn in enumerate(cols):
        rank = jnp.zeros(gn.shape, F32)
        for m_, gm in enumerate(cols):
            if m_ == n:
                continue
            ahead = (gm >= gn) if m_ < n else (gm > gn)
            rank = rank + jnp.where(ahead, 1.0, 0.0)
        sels.append(rank < float(k))
    return sels


def _moba_prompt_kernel(rb_ref, bkt_ref, q_ref, k_ref, v_ref, o_ref, bias_ref):
    h = pl.program_id(0)
    s_len = q_ref.shape[1]
    blk = MOBA_BLOCK
    nb = s_len // blk
    scale = HEAD_DIM ** -0.5

    @pl.when(pl.program_id(1) == 0)
    def _():
        row = lax.broadcasted_iota(I32, (blk, blk), 0)
        col = lax.broadcasted_iota(I32, (blk, blk), 1)
        for t in range(2):
            bkt = bkt_ref[t]
            bias = jnp.zeros((blk, blk), F32)
            for r in range(REL_BUCKETS):
                bias = jnp.where(bkt == r, rb_ref[r, h], bias)
            bias_ref[t] = jnp.where(col <= row, bias, NEG_INF) if t == 0 else bias

    bias_far = rb_ref[REL_BUCKETS - 1, h]
    k_means = [jnp.mean(k_ref[0, n * blk:(n + 1) * blk, :], axis=0, keepdims=True) for n in range(nb)]

    for qb in range(nb):
        q = q_ref[0, qb * blk:(qb + 1) * blk, :]
        qs = (q * scale).astype(BF16)
        if qb > MOBA_TOPK:
            gates = [jnp.sum(q * k_means[n], axis=-1, keepdims=True) for n in range(qb)]
            masks = [jnp.where(sel, 0.0, NEG_INF) for sel in _topk_select(gates, MOBA_TOPK)]
        else:
            masks = [None] * qb
        s = _dot_nt(qs, k_ref[0, qb * blk:(qb + 1) * blk, :]) + bias_ref[0]
        m = jnp.max(s, axis=-1, keepdims=True)
        p = jnp.exp(s - m)
        l = jnp.sum(p, axis=-1, keepdims=True)
        acc = _dot(p, v_ref[0, qb * blk:(qb + 1) * blk, :])
        for n in range(qb - 1, -1, -1):
            s = _dot_nt(qs, k_ref[0, n * blk:(n + 1) * blk, :])
            if n == qb - 1:
                s = s + bias_ref[1]
                if masks[n] is not None:
                    s = s + masks[n]
            else:
                s = s + (bias_far if masks[n] is None else masks[n] + bias_far)
            m, l, acc = _softmax_step(s, v_ref[0, n * blk:(n + 1) * blk, :], m, l, acc)
        o_ref[0, qb * blk:(qb + 1) * blk, :] = acc / l


def moba_prompt(q, k, v, rel_bias):
    b, s_len, width = q.shape
    assert s_len % MOBA_BLOCK == 0 and width == A_HEADS * HEAD_DIM
    ar = np.arange(MOBA_BLOCK)
    d_loc = ar[:, None] - ar[None, :]
    bkt = np.stack([_rel_bucket_np(d_loc), _rel_bucket_np(d_loc + MOBA_BLOCK)]).astype(np.int32)
    assert int(_rel_bucket_np(np.array([MOBA_BLOCK + 1]))[0]) == REL_BUCKETS - 1
    spec = pl.BlockSpec((1, s_len, HEAD_DIM), lambda h, i: (i, 0, h))
    return pl.pallas_call(
        _moba_prompt_kernel,
        out_shape=jax.ShapeDtypeStruct((b, s_len, width), F32),
        grid=(A_HEADS, b),
        in_specs=[pl.BlockSpec(memory_space=pltpu.SMEM),
                  pl.BlockSpec((2, MOBA_BLOCK, MOBA_BLOCK), lambda h, i: (0, 0, 0)),
                  spec, spec, spec],
        out_specs=spec,
        scratch_shapes=[pltpu.VMEM((2, MOBA_BLOCK, MOBA_BLOCK), F32)],
        compiler_params=_cparams(("arbitrary", "arbitrary")),
        name="moba_prompt",
    )(rel_bias, jnp.asarray(bkt), q, k, v)


def _moba_sample_kernel(pt_ref, rb_ref, bkt_ref, q_ref, kn_ref, vn_ref, *rest, past_len):
    nh = A_HEADS
    n_pg = 2 * SAMPLE_BLOCKS
    k_pages_all = rest[0:n_pg]
    v_pages_all = rest[n_pg:2 * n_pg]
    o_ref, bias_ref, m_ref, l_ref, g_ref, acc_ref = rest[2 * n_pg:]
    page = k_pages_all[0].shape[1] // nh

    def head_rows(ref, h):
        return ref[0, pl.ds(h, page, stride=nh), :]

    n = pl.program_id(1)
    nb = pl.num_programs(1)
    t = q_ref.shape[1]
    dh = HEAD_DIM
    blk = MOBA_BLOCK
    scale = dh ** -0.5
    n_last = past_len // blk - 1

    @pl.when(n == 0)
    def _():
        for h in range(nh):
            bkt = bkt_ref[...]
            bias = jnp.zeros((t, blk), F32)
            for r in range(REL_BUCKETS):
                bias = jnp.where(bkt == r, rb_ref[r, h], bias)
            bias_ref[h * t:(h + 1) * t, :] = bias

    q_all = q_ref[0]
    qhs = [q_all[:, h * dh:(h + 1) * dh] for h in range(nh)]
    qss = [(qh * scale).astype(BF16) for qh in qhs]
    heads = range(nh)
    for jb in range(SAMPLE_BLOCKS):
        gi = n * SAMPLE_BLOCKS + jb
        k_pages = k_pages_all[2 * jb:2 * jb + 2]
        v_pages = v_pages_all[2 * jb:2 * jb + 2]
        k_blks = [jnp.concatenate([head_rows(k_pages[0], h), head_rows(k_pages[1], h)], axis=0) for h in heads]
        v_blks = [jnp.concatenate([head_rows(v_pages[0], h), head_rows(v_pages[1], h)], axis=0) for h in heads]
        ss = [_dot_nt(qss[h], k_blks[h]) for h in heads]
        gs = [jnp.sum(qhs[h] * jnp.mean(k_blks[h], axis=0, keepdims=True), axis=-1, keepdims=True)
              for h in heads]
        ss = [ss[h] + jnp.where(gi == n_last, bias_ref[h * t:(h + 1) * t, :], rb_ref[REL_BUCKETS - 1, h])
              for h in heads]
        ms = [jnp.max(s, axis=-1, keepdims=True) for s in ss]
        ps = [jnp.exp(s - m) for s, m in zip(ss, ms)]
        accs = [_dot(ps[h], v_blks[h]) for h in heads]
        m_ref[gi] = jnp.concatenate(ms, axis=0)
        l_ref[gi] = jnp.concatenate([jnp.sum(p, axis=-1, keepdims=True) for p in ps], axis=0)
        g_ref[gi] = jnp.concatenate(gs, axis=0)
        acc_ref[gi] = jnp.concatenate(accs, axis=0)

    @pl.when(n == nb - 1)
    def _():
        n_blocks = past_len // blk
        rows = nh * t
        sels = _topk_select([g_ref[i] for i in range(n_blocks)], MOBA_TOPK)
        tq = lax.broadcasted_iota(I32, (t, t), 0)
        tk = lax.broadcasted_iota(I32, (t, t), 1)
        s_locs = []
        for h in range(nh):
            bias_loc = jnp.zeros((t, t), F32)
            for d in range(t):
                bias_loc = jnp.where(tq - tk == d, rb_ref[d, h], bias_loc)
            s_loc = _dot_nt(qss[h], kn_ref[0, :, h * dh:(h + 1) * dh]) + bias_loc
            s_locs.append(jnp.where(tk <= tq, s_loc, NEG_INF))
        s_loc = jnp.concatenate(s_locs, axis=0)
        m_tot = jnp.max(s_loc, axis=-1, keepdims=True)
        for i in range(n_blocks):
            m_tot = jnp.maximum(m_tot, jnp.where(sels[i], m_ref[i], NEG_INF))
        p_loc = jnp.exp(s_loc - m_tot)
        l_tot = jnp.sum(p_loc, axis=-1, keepdims=True)
        acc = jnp.concatenate([_dot(p_loc[h * t:(h + 1) * t, :], vn_ref[0, :, h * dh:(h + 1) * dh])
                               for h in range(nh)], axis=0)
        for i in range(n_blocks):
            w = jnp.where(sels[i], jnp.exp(jnp.where(sels[i], m_ref[i] - m_tot, 0.0)), 0.0)
            l_tot = l_tot + w * l_ref[i]
            acc = acc + w * acc_ref[i]
        out = acc / l_tot
        o_ref[0] = jnp.concatenate([out[h * t:(h + 1) * t, :] for h in range(nh)], axis=1)


def moba_sample(q, k_new, v_new, cache_k, cache_v, layer, page_table, rel_bias, past_len):
    db, t, width = q.shape
    page = cache_k.shape[2]
    assert MOBA_BLOCK == 2 * page and past_len % MOBA_BLOCK == 0 and t <= REL_BUCKETS // 2
    assert past_len // MOBA_BLOCK >= MOBA_TOPK and t % SUBLANES == 0
    n_blocks = past_len // MOBA_BLOCK
    assert n_blocks % SAMPLE_BLOCKS == 0
    rows = A_HEADS * t
    d_last = MOBA_BLOCK + np.arange(t)[:, None] - np.arange(MOBA_BLOCK)[None, :]
    bkt = _rel_bucket_np(d_last).astype(np.int32)
    assert int(_rel_bucket_np(np.array([MOBA_BLOCK + 1]))[0]) == REL_BUCKETS - 1
    tok_spec = pl.BlockSpec((1, t, width), lambda i, n, pt: (i, 0, 0))

    n_phys = cache_k.shape[1]
    ck = cache_k.reshape(cache_k.shape[0] * n_phys, page * A_HEADS, HEAD_DIM)
    cv = cache_v.reshape(cache_v.shape[0] * n_phys, page * A_HEADS, HEAD_DIM)

    n_pg = 2 * SAMPLE_BLOCKS
    page_specs = [pl.BlockSpec((1, page * A_HEADS, HEAD_DIM),
                               lambda i, n, pt, j=j: (layer * n_phys + pt[i, n_pg * n + j], 0, 0))
                  for j in range(n_pg)]

    grid_spec = pltpu.PrefetchScalarGridSpec(
        num_scalar_prefetch=1,
        grid=(db, n_blocks // SAMPLE_BLOCKS),
        in_specs=[pl.BlockSpec(memory_space=pltpu.SMEM),
                  pl.BlockSpec((t, MOBA_BLOCK), lambda i, n, pt: (0, 0)),
                  tok_spec, tok_spec, tok_spec] + page_specs + page_specs,
        out_specs=tok_spec,
        scratch_shapes=[pltpu.VMEM((rows, MOBA_BLOCK), F32),
                        pltpu.VMEM((n_blocks, rows, 1), F32),
                        pltpu.VMEM((n_blocks, rows, 1), F32),
                        pltpu.VMEM((n_blocks, rows, 1), F32),
                        pltpu.VMEM((n_blocks, rows, HEAD_DIM), F32)],
    )
    return pl.pallas_call(
        functools.partial(_moba_sample_kernel, past_len=past_len),
        out_shape=jax.ShapeDtypeStruct((db, t, width), F32),
        grid_spec=grid_spec,
        compiler_params=_cparams(("arbitrary", "arbitrary")),
        name="moba_sample",
    )(page_table, rel_bias, jnp.asarray(bkt), q, k_new, v_new, *([ck] * n_pg + [cv] * n_pg))


INV_BASE = 16


def _unit_lower_inverses(a_mats, ri, ci, size):
    base = min(INV_BASE, size)
    same = ri // base == ci // base
    eye = jnp.where(ri == ci, 1.0, 0.0)
    pws = [jnp.where(same, a, 0.0) for a in a_mats]
    invs = [eye - p for p in pws]
    for _ in range(max(int(math.log2(base)) - 1, 0)):
        pws = [_mm(p, p, "nn", 1) for p in pws]
        invs = [i + _mm(i, p, "nn", 1) for i, p in zip(invs, pws)]
    blk = base
    while blk < size:
        pr = ri // blk
        pc = ci // blk
        join = (pr == pc + 1) & (pr // 2 == pc // 2)
        halves = [_mm(i, jnp.where(join, a, 0.0), "nn", 1) for i, a in zip(invs, a_mats)]
        invs = [i - _mm(hf, i, "nn", 1) for i, hf in zip(invs, halves)]
        blk *= 2
    return invs


def _delta_kernel(x_ref, cw_ref, z_ref, sm_ref, smt_ref, ar_ref, dr_ref, ac_ref, dc_ref, nw_ref, s0_ref, b0_ref,
                  y_ref, sn_ref, tail_ref, s_sc, carry_sc):
    c = pl.program_id(1)
    nc = pl.num_programs(1)
    chunk = x_ref.shape[1]
    dk = HEAD_DIM
    hh = G_HEADS
    part = hh * dk

    @pl.when(c == 0)
    def _():
        s_sc[...] = s0_ref[0]
        carry_sc[...] = b0_ref[0]

    x = x_ref[0]
    xx = jnp.concatenate([carry_sc[...], x], axis=0)
    cw = cw_ref[...]
    y = x * cw[CONV_W - 1:CONV_W, :]
    for s in range(1, CONV_W):
        y = y + xx[SUBLANES - s:SUBLANES - s + chunk, :] * cw[CONV_W - 1 - s:CONV_W - s, :]
    carry_sc[...] = xx[chunk:chunk + SUBLANES, :]
    qkv = _silu(y)

    def softplus(t):
        return jnp.maximum(t, 0.0) + jnp.log(1.0 + jnp.exp(-jnp.abs(t)))

    sm = sm_ref[0]
    beta_all = jax.nn.sigmoid(sm)
    g_cols = -jnp.exp(ar_ref[...]) * softplus(sm + dr_ref[...])
    g_rows = -jnp.exp(ac_ref[...]) * softplus(smt_ref[0] + dc_ref[...])
    ri = lax.broadcasted_iota(I32, (chunk, chunk), 0)
    ci = lax.broadcasted_iota(I32, (chunk, chunk), 1)
    incl = ci <= ri
    strict = ci < ri
    gc_cols = _dot_exact(jnp.where(incl, 1.0, 0.0), g_cols)
    gc_rows = _dot_exact(g_rows, jnp.where(ri <= ci, 1.0, 0.0))
    nw = nw_ref[...]

    heads = range(hh)
    qs, ks, vs, betas, gcs, dmasks = [], [], [], [], [], []
    for h in heads:
        q = qkv[:, h * dk:(h + 1) * dk]
        k = qkv[:, part + h * dk:part + (h + 1) * dk]
        qs.append(q * lax.rsqrt(jnp.sum(q * q, axis=-1, keepdims=True) + EPS) * (dk ** -0.5))
        ks.append(k * lax.rsqrt(jnp.sum(k * k, axis=-1, keepdims=True) + EPS))
        vs.append(qkv[:, 2 * part + h * dk:2 * part + (h + 1) * dk])
        betas.append(beta_all[:, h:h + 1])
        gc_c = gc_cols[:, hh + h:hh + h + 1]
        gc_r = gc_rows[hh + h:hh + h + 1, :]
        gcs.append(gc_c)
        dmasks.append(jnp.exp(jnp.where(incl, gc_c - gc_r, NEG_INF)))

    kbs = [k * b for k, b in zip(ks, betas)]
    a_mats = [jnp.where(strict, _mm(kb, k, "nt", 1) * dm, 0.0) for kb, k, dm in zip(kbs, ks, dmasks)]
    t_invs = _unit_lower_inverses(a_mats, ri, ci, chunk)
    egs = [jnp.exp(g) for g in gcs]
    uws = [_mm(t, jnp.concatenate([v * b, kb * eg], axis=1), "nn", 1)
           for t, v, b, kb, eg in zip(t_invs, vs, betas, kbs, egs)]
    s_mats = [s_sc[h] for h in heads]
    v_news = [uw[:, :dk] - _mm(uw[:, dk:], s, "nn", 1) for uw, s in zip(uws, s_mats)]
    attns = [_mm(q, k, "nt", 1) * dm for q, k, dm in zip(qs, ks, dmasks)]
    outs = [_mm(q * eg, s, "nn", 1) + _mm(at, vn, "nn", 1)
            for q, eg, s, at, vn in zip(qs, egs, s_mats, attns, v_news)]
    for h in heads:
        g_last = gcs[h][chunk - 1:chunk, :]
        s_sc[h] = s_mats[h] * jnp.exp(g_last) + _mm(ks[h] * jnp.exp(g_last - gcs[h]), v_news[h], "tn", 1)
    for h in heads:
        o = outs[h]
        o_n = o * lax.rsqrt(jnp.mean(o * o, axis=-1, keepdims=True) + EPS) * nw
        y_ref[0, :, h * dk:(h + 1) * dk] = o_n * _silu(z_ref[0, :, h * dk:(h + 1) * dk])

    @pl.when(c == nc - 1)
    def _():
        sn_ref[0] = s_sc[...]
        tail_ref[0] = carry_sc[...]


def delta_rule(xg, z, small, small_t, conv_w, a_log, dt_bias, delta_norm, s0, buf0, chunk):
    b, l, cw_ch = xg.shape
    hh = G_HEADS
    assert l % chunk == 0 and cw_ch == 3 * hh * HEAD_DIM and l >= CONV_W - 1
    nc = l // chunk
    width = hh * HEAD_DIM
    a_row = jnp.pad(a_log.reshape(1, hh), ((0, 0), (hh, LANES - 2 * hh)))
    d_row = jnp.pad(dt_bias.reshape(1, hh), ((0, 0), (hh, LANES - 2 * hh)))
    a_col = jnp.pad(a_log.reshape(hh, 1), ((hh, 0), (0, 0)))
    d_col = jnp.pad(dt_bias.reshape(hh, 1), ((hh, 0), (0, 0)))

    def const(shape):
        return pl.BlockSpec(shape, lambda i, c: (0,) * len(shape))

    return pl.pallas_call(
        _delta_kernel,
        out_shape=[jax.ShapeDtypeStruct((b, l, width), F32),
                   jax.ShapeDtypeStruct((b, hh, HEAD_DIM, HEAD_DIM), F32),
                   jax.ShapeDtypeStruct((b, SUBLANES, cw_ch), F32)],
        grid=(b, nc),
        in_specs=[pl.BlockSpec((1, chunk, cw_ch), lambda i, c: (i, c, 0)),
                  const((CONV_W, cw_ch)),
                  pl.BlockSpec((1, chunk, width), lambda i, c: (i, c, 0)),
                  pl.BlockSpec((1, chunk, LANES), lambda i, c: (i, c, 0)),
                  pl.BlockSpec((1, 2 * hh, chunk), lambda i, c: (i, 0, c)),
                  const((1, LANES)), const((1, LANES)), const((2 * hh, 1)), const((2 * hh, 1)),
                  const((1, HEAD_DIM)),
                  pl.BlockSpec((1, hh, HEAD_DIM, HEAD_DIM), lambda i, c: (i, 0, 0, 0)),
                  pl.BlockSpec((1, SUBLANES, cw_ch), lambda i, c: (i, 0, 0))],
        out_specs=[pl.BlockSpec((1, chunk, width), lambda i, c: (i, c, 0)),
                   pl.BlockSpec((1, hh, HEAD_DIM, HEAD_DIM), lambda i, c: (i, 0, 0, 0)),
                   pl.BlockSpec((1, SUBLANES, cw_ch), lambda i, c: (i, 0, 0))],
        scratch_shapes=[pltpu.VMEM((hh, HEAD_DIM, HEAD_DIM), F32),
                        pltpu.VMEM((SUBLANES, cw_ch), F32)],
        compiler_params=_cparams(("arbitrary", "arbitrary")),
        name="delta_rule",
    )(xg, conv_w, z, small, small_t, a_row, d_row, a_col, d_col, delta_norm.reshape(1, HEAD_DIM), s0, buf0)


def _post_kernel(ya_ref, yb_ref, ga_ref, gb_ref, x_ref, gm_ref, scf_ref, shf_ref, wa_ref, wb_ref, wo_ref,
                 nw_ref, rw_ref, x1_ref, h2_ref, lg_ref):
    bt, tl, d = x_ref.shape
    rows = bt * tl
    ya = ya_ref[...].reshape(rows, -1)
    yb = yb_ref[...].reshape(rows, -1)
    merged = (jax.nn.sigmoid(ga_ref[...].reshape(rows, d)) * _dot(ya, wa_ref[...])
              + jax.nn.sigmoid(gb_ref[...].reshape(rows, d)) * _dot(yb, wb_ref[...]))
    mix = _dot(merged, wo_ref[...]).reshape(bt, tl, d)
    x1 = x_ref[...] + gm_ref[:, 0] * mix
    x1_ref[...] = x1
    y = x1 * lax.rsqrt(jnp.mean(x1 * x1, axis=-1, keepdims=True) + EPS) * nw_ref[...]
    h2 = (y * (1.0 + scf_ref[:, 0]) + shf_ref[:, 0]).reshape(rows, d)
    h2_ref[...] = _pack_pairs(h2).reshape(bt, tl, d // 2)
    lg_ref[...] = _mm(h2, rw_ref[...], "nn", 3).reshape(bt, tl, -1)


def post_mix(y_a, y_b, gate_a, gate_b, x, mod4, w_a, w_b, w_o, norm_w, router_w, bt, tl):
    b, l, d = x.shape
    wa_in = y_a.shape[-1]
    wb_in = y_b.shape[-1]
    n_e = router_w.shape[1]

    def tok(width):
        return pl.BlockSpec((bt, tl, width), lambda i, t: (i, t, 0))

    def modspec(idx):
        return pl.BlockSpec((bt, 1, 1, d), lambda i, t, idx=idx: (i, idx, 0, 0))

    def const(shape):
        return pl.BlockSpec(shape, lambda i, t: (0, 0), pipeline_mode=pl.Buffered(1))

    return pl.pallas_call(
        _post_kernel,
        out_shape=[jax.ShapeDtypeStruct((b, l, d), F32), jax.ShapeDtypeStruct((b, l, d // 2), jnp.uint32),
                   jax.ShapeDtypeStruct((b, l, n_e), F32)],
        grid=(b // bt, l // tl),
        in_specs=[tok(wa_in), tok(wb_in), tok(d), tok(d), tok(d), modspec(2), modspec(4), modspec(3),
                  const((wa_in, d)), const((wb_in, d)), const((d, d)), const((1, d)), const((d, n_e))],
        out_specs=[tok(d), tok(d // 2), tok(n_e)],
        compiler_params=_cparams(("arbitrary", "arbitrary")),
        name="post_mix",
    )(y_a, y_b, gate_a, gate_b, x, mod4, mod4, mod4, w_a, w_b, w_o, norm_w.reshape(1, d), router_w)


def _route_kernel(lg_ref, rb_ref, e_ref, w_ref, p_ref, cnt_ref, run_sc, *, n_e):
    i = pl.program_id(0)
    tm, width = lg_ref.shape
    per_group = n_e // N_GROUPS

    @pl.when(i == 0)
    def _():
        run_sc[...] = jnp.zeros_like(run_sc)

    lane = lax.broadcasted_iota(I32, (tm, width), 1)
    scores = jnp.where(lane < n_e, jax.nn.sigmoid(lg_ref[...]), 0.0)
    biased = jnp.where(lane < n_e, scores + rb_ref[...], NEG_INF)
    grp = lane // per_group

    def first_argmax(vals):
        mx = jnp.max(vals, axis=-1, keepdims=True)
        idx = jnp.min(jnp.where(vals == mx, lane, width), axis=-1, keepdims=True)
        return mx, idx

    group_scores = []
    for g in range(N_GROUPS):
        mg = jnp.where(grp == g, biased, NEG_INF)
        m1, i1 = first_argmax(mg)
        m2 = jnp.max(jnp.where(lane == i1, NEG_INF, mg), axis=-1, keepdims=True)
        group_scores.append(m1 + m2)
    keep = _topk_select(group_scores, TOPK_GROUPS)
    expert_keep = jnp.zeros((tm, width), jnp.bool_)
    for g in range(N_GROUPS):
        expert_keep = expert_keep | ((grp == g) & keep[g])
    masked = jnp.where(expert_keep, biased, NEG_INF)

    sel = jnp.zeros((tm, width), jnp.bool_)
    idxs = []
    for _ in range(TOP_K):
        _, idx = first_argmax(masked)
        hit = lane == idx
        sel = sel | hit
        masked = jnp.where(hit, NEG_INF, masked)
        idxs.append(idx)
    sel_f = jnp.where(sel, 1.0, 0.0)
    top_w = scores * sel_f
    top_w = top_w / jnp.sum(top_w, axis=-1, keepdims=True) * ROUTED_SCALE

    ri = lax.broadcasted_iota(I32, (tm, tm), 0)
    ci = lax.broadcasted_iota(I32, (tm, tm), 1)
    before = _dot(jnp.where(ci < ri, 1.0, 0.0), sel_f) + run_sc[...]
    run_sc[...] = run_sc[...] + jnp.sum(sel_f, axis=0, keepdims=True)
    cnt_ref[...] = run_sc[...].astype(I32)

    lane_o = lax.broadcasted_iota(I32, (tm, LANES), 1)
    e_out = jnp.zeros((tm, LANES), I32)
    w_out = jnp.zeros((tm, LANES), F32)
    p_out = jnp.zeros((tm, LANES), I32)
    for kk, idx in enumerate(idxs):
        hit = lane == idx
        wk = jnp.sum(jnp.where(hit, top_w, 0.0), axis=-1, keepdims=True)
        pk = jnp.sum(jnp.where(hit, before, 0.0), axis=-1, keepdims=True).astype(I32)
        e_out = jnp.where(lane_o == kk, idx, e_out)
        w_out = jnp.where(lane_o == kk, wk, w_out)
        p_out = jnp.where(lane_o == kk, pk, p_out)
    e_ref[...] = e_out
    w_ref[...] = w_out
    p_ref[...] = p_out


def route(logits, router_bias, n_e, tm):
    n, width = logits.shape
    assert n % tm == 0 and width == LANES and n_e <= LANES
    tok = pl.BlockSpec((tm, LANES), lambda i: (i, 0))
    one = pl.BlockSpec((1, LANES), lambda i: (0, 0))
    bias = jnp.pad(router_bias.reshape(1, n_e), ((0, 0), (0, LANES - n_e)))
    return pl.pallas_call(
        functools.partial(_route_kernel, n_e=n_e),
        out_shape=[jax.ShapeDtypeStruct((n, LANES), I32), jax.ShapeDtypeStruct((n, LANES), F32),
                   jax.ShapeDtypeStruct((n, LANES), I32), jax.ShapeDtypeStruct((1, LANES), I32)],
        grid=(n // tm,),
        in_specs=[tok, one],
        out_specs=[tok, tok, tok, one],
        scratch_shapes=[pltpu.VMEM((1, LANES), F32)],
        compiler_params=_cparams(("arbitrary",)),
        name="route",
    )(logits, bias)


def _row_copy(src_ref, dst_ref, src_row, dst_row, sem):
    return pltpu.make_async_copy(src_ref.at[pl.ds(src_row, 1)], dst_ref.at[pl.ds(dst_row, 1)], sem)


def _dispatch_kernel(dest_ref, h_ref, hs_in_ref, hs_ref, sem):
    del hs_in_ref
    tm = h_ref.shape[0]

    def body(t, carry):
        for kk in range(TOP_K):
            _row_copy(h_ref, hs_ref, t, dest_ref[0, 0, t * TOP_K + kk], sem).start()
        return carry

    def drain(t, carry):
        for kk in range(TOP_K):
            _row_copy(h_ref, hs_ref, t, dest_ref[0, 0, t * TOP_K + kk], sem).wait()
        return carry

    lax.fori_loop(0, tm, body, 0, unroll=4)
    lax.fori_loop(0, tm, drain, 0, unroll=4)


def dispatch(h_all, dest, n_rows, tm):
    n, d = h_all.shape
    assert n % tm == 0
    dest3 = dest.reshape(n // tm, 1, tm * TOP_K)
    zeros = jnp.zeros((n_rows, d), h_all.dtype)
    return pl.pallas_call(
        _dispatch_kernel,
        out_shape=jax.ShapeDtypeStruct((n_rows, d), h_all.dtype),
        grid=(n // tm,),
        in_specs=[pl.BlockSpec((1, 1, tm * TOP_K), lambda i: (i, 0, 0), memory_space=pltpu.SMEM),
                  pl.BlockSpec((tm, d), lambda i: (i, 0)),
                  pl.BlockSpec(memory_space=pl.ANY)],
        out_specs=pl.BlockSpec(memory_space=pl.ANY),
        scratch_shapes=[pltpu.SemaphoreType.DMA(())],
        input_output_aliases={2: 0},
        compiler_params=_cparams(("arbitrary",)),
        name="dispatch",
    )(dest3, h_all, zeros)


def _experts_kernel(be_ref, bf_ref, nu_ref, x_ref, wg_ref, wu_ref, wd_ref, y_ref, wg_sc, wu_sc, wd_sc):
    i = pl.program_id(0)

    @pl.when(bf_ref[i] == 1)
    def _():
        wg_sc[...] = wg_ref[0].astype(BF16)
        wu_sc[...] = wu_ref[0].astype(BF16)
        wd_sc[...] = wd_ref[0].astype(BF16)

    @pl.when(i < nu_ref[0])
    def _():
        y_ref[...] = _pack_pairs(_swiglu_packed(x_ref[...], wg_sc, wu_sc, wd_sc))

    @pl.when(i >= nu_ref[0])
    def _():
        y_ref[...] = jnp.zeros_like(y_ref)


def _swiglu_packed(xp, wg_ref, wu_ref, wd_ref):
    lo, hi = _unpack_pairs(xp)
    lo = lo.astype(BF16)
    hi = hi.astype(BF16)
    half = xp.shape[-1]

    def proj(w_ref):
        return (jnp.dot(lo, w_ref[:half, :], preferred_element_type=F32)
                + jnp.dot(hi, w_ref[half:, :], preferred_element_type=F32))

    act = (_silu(proj(wg_ref)) * proj(wu_ref)).astype(BF16)
    return jnp.dot(act, wd_ref[...], preferred_element_type=F32)


def experts(h_sorted, blk_e, blk_first, n_used, w_gate, w_up, w_down):
    n_rows, dp = h_sorted.shape
    n_blocks = n_rows // EXPERT_ROWS
    d, de = w_gate.shape[-2:]
    assert d == 2 * dp
    grid_spec = pltpu.PrefetchScalarGridSpec(
        num_scalar_prefetch=3,
        grid=(n_blocks,),
        in_specs=[pl.BlockSpec((EXPERT_ROWS, dp), lambda i, be, bf, nu: (i, 0)),
                  pl.BlockSpec((1, d, de), lambda i, be, bf, nu: (be[i], 0, 0)),
                  pl.BlockSpec((1, d, de), lambda i, be, bf, nu: (be[i], 0, 0)),
                  pl.BlockSpec((1, de, d), lambda i, be, bf, nu: (be[i], 0, 0))],
        out_specs=pl.BlockSpec((EXPERT_ROWS, dp), lambda i, be, bf, nu: (i, 0)),
        scratch_shapes=[pltpu.VMEM((d, de), BF16), pltpu.VMEM((d, de), BF16), pltpu.VMEM((de, d), BF16)],
    )
    return pl.pallas_call(
        _experts_kernel,
        out_shape=jax.ShapeDtypeStruct((n_rows, dp), jnp.uint32),
        grid_spec=grid_spec,
        compiler_params=_cparams(("arbitrary",)),
        name="experts",
    )(blk_e, blk_first, n_used, h_sorted, w_gate, w_up, w_down)


def _combine_kernel(dest_ref, x1_ref, h2_ref, gf_ref, w_ref, wg_ref, wu_ref, wd_ref, nw_ref, yr_ref, o_ref,
                    gbuf, sem):
    bt, tl, d = x1_ref.shape
    tm = bt * tl

    def body(t, carry):
        for kk in range(TOP_K):
            _row_copy(yr_ref, gbuf.at[kk], dest_ref[0, 0, t * TOP_K + kk], t, sem).start()
        return carry

    lax.fori_loop(0, tm, body, 0, unroll=4)
    y = _swiglu_packed(h2_ref[...].reshape(tm, d // 2), wg_ref, wu_ref, wd_ref)
    wts = w_ref[...]

    def drain(t, carry):
        for kk in range(TOP_K):
            _row_copy(yr_ref, gbuf.at[kk], dest_ref[0, 0, t * TOP_K + kk], t, sem).wait()
        return carry

    lax.fori_loop(0, tm, drain, 0)
    r_lo = jnp.zeros((tm, d // 2), F32)
    r_hi = jnp.zeros((tm, d // 2), F32)
    for kk in range(TOP_K):
        lo, hi = _unpack_pairs(gbuf[kk])
        r_lo = r_lo + lo * wts[:, kk:kk + 1]
        r_hi = r_hi + hi * wts[:, kk:kk + 1]
    y = y + jnp.concatenate([r_lo, r_hi], axis=1)
    x2 = x1_ref[...] + gf_ref[:, 0] * y.reshape(bt, tl, d)
    o_ref[...] = x2 * lax.rsqrt(jnp.mean(x2 * x2, axis=-1, keepdims=True) + EPS) * nw_ref[...]


def combine(x1, h2, mod4, sel_w, dest, y_rows, w_gs, w_us, w_ds, norm_final, bt, tl):
    b, l, d = x1.shape
    tm = bt * tl
    n = b * l
    ds = w_gs.shape[1]
    n_t = l // tl
    dest3 = dest.reshape(n // tm, 1, tm * TOP_K)

    def const(shape):
        return pl.BlockSpec(shape, lambda i, t: (0, 0), pipeline_mode=pl.Buffered(1))

    return pl.pallas_call(
        _combine_kernel,
        out_shape=jax.ShapeDtypeStruct((b, l, d), F32),
        grid=(b // bt, n_t),
        in_specs=[pl.BlockSpec((1, 1, tm * TOP_K), lambda i, t: (i * n_t + t, 0, 0), memory_space=pltpu.SMEM),
                  pl.BlockSpec((bt, tl, d), lambda i, t: (i, t, 0)),
                  pl.BlockSpec((bt, tl, d // 2), lambda i, t: (i, t, 0)),
                  pl.BlockSpec((bt, 1, 1, d), lambda i, t: (i, 5, 0, 0)),
                  pl.BlockSpec((tm, LANES), lambda i, t: (i * n_t + t, 0)),
                  const((d, ds)), const((d, ds)), const((ds, d)), const((1, d)),
                  pl.BlockSpec(memory_space=pl.ANY)],
        out_specs=pl.BlockSpec((bt, tl, d), lambda i, t: (i, t, 0)),
        scratch_shapes=[pltpu.VMEM((TOP_K, tm, d // 2), jnp.uint32), pltpu.SemaphoreType.DMA(())],
        compiler_params=_cparams(("arbitrary", "arbitrary")),
        name="combine",
    )(dest3, x1, h2, mod4, sel_w, w_gs, w_us, w_ds, norm_final.reshape(1, d), y_rows)


def _moe(h2_p, h2_s, x1_p, x1_s, lg_p, lg_s, mod_p, mod_s, router_bias, w_gate_e, w_up_e, w_down_e,
         w_gs, w_us, w_ds, norm_final, tiles_p, tiles_s):
    d = h2_p.shape[-1]
    n_p = h2_p.shape[0] * h2_p.shape[1]
    n_s = h2_s.shape[0] * h2_s.shape[1]
    n = n_p + n_s
    n_e = w_gate_e.shape[0]
    h_all = jnp.concatenate([h2_p.reshape(n_p, d), h2_s.reshape(n_s, d)], axis=0)
    logits = jnp.concatenate([lg_p.reshape(n_p, LANES), lg_s.reshape(n_s, LANES)], axis=0)
    sel_e, sel_w, sel_pos, counts = route(logits, router_bias, n_e, 512)
    counts = counts[0, :n_e]
    padded = (counts + EXPERT_ROWS - 1) // EXPERT_ROWS * EXPERT_ROWS
    pad_ends = jnp.cumsum(padded)
    pad_starts = pad_ends - padded
    n_blocks = -(-(n * TOP_K + n_e * (EXPERT_ROWS - 1)) // EXPERT_ROWS)
    n_rows = n_blocks * EXPERT_ROWS
    dest = pad_starts[sel_e[:, :TOP_K]] + sel_pos[:, :TOP_K]
    blk_start = jnp.arange(n_blocks, dtype=I32) * EXPERT_ROWS
    n_used = (pad_ends[-1] // EXPERT_ROWS).astype(I32)
    blk_e = jnp.minimum(jnp.sum((blk_start[:, None] >= pad_ends[None, :]).astype(I32), axis=1), n_e - 1)
    last_used_e = blk_e[jnp.maximum(n_used - 1, 0)]
    blk_e = jnp.where(jnp.arange(n_blocks) < n_used, blk_e, last_used_e)
    blk_first = jnp.concatenate([jnp.ones((1,), I32), (blk_e[1:] != blk_e[:-1]).astype(I32)])
    h_sorted = dispatch(h_all, dest, n_rows, 512)
    y_rows = experts(h_sorted, blk_e, blk_first, n_used.reshape(1), w_gate_e, w_up_e, w_down_e)
    y_p = combine(x1_p, h2_p, mod_p, sel_w[:n_p], dest[:n_p], y_rows, w_gs, w_us, w_ds, norm_final, *tiles_p)
    y_s = combine(x1_s, h2_s, mod_s, sel_w[n_p:], dest[n_p:], y_rows, w_gs, w_us, w_ds, norm_final, *tiles_s)
    return y_p, y_s


def kernel(x_prompt, x_sample, cache_k, cache_v, state_delta, state_conv, page_table, c_prompt, c_sample,
           w_ada, b_ada, norm_mix, w_in, conv_w, a_log, dt_bias, delta_norm, w_branch_a, w_branch_b, w_out,
           rel_bias, norm_ffn, router_w, router_bias, w_gate_e, w_up_e, w_down_e, w_gate_s, w_up_s, w_down_s,
           norm_final):
    depth = w_ada.shape[0]
    assert depth == 1, "the final norm is fused into the layer's last stage"
    n_b, seq, d = x_prompt.shape
    d_b, d_seq, _ = x_sample.shape
    n_pages = page_table.shape[1]
    page = cache_k.shape[2]
    past_len = n_pages * page
    a_width = A_HEADS * HEAD_DIM
    g_width = G_HEADS * HEAD_DIM
    conv_ch = 3 * g_width
    l = 0

    off_small = 3 * a_width + conv_ch + g_width
    w_l = w_in[l]
    w_main = jnp.concatenate([w_l[:, :off_small], w_l[:, off_small + 2 * G_HEADS:]], axis=1).astype(BF16)
    w_small = jnp.pad(w_l[:, off_small:off_small + 2 * G_HEADS], ((0, 0), (0, LANES - 2 * G_HEADS))).astype(BF16)
    assert w_main.shape[1] == IN_COL_TILES * IN_TILE

    c_all = jnp.concatenate([c_prompt, c_sample], axis=0)
    mod = ada_mod(c_all, w_ada[l], b_ada[l])
    mod_p = mod[:n_b].reshape(n_b, N_ADA, 1, d)
    mod_s = mod[n_b:].reshape(d_b, N_ADA, 1, d)

    tiles_p = (1, 512)
    tiles_s = (512 // d_seq, d_seq)
    post_p = (1, 256)
    post_s = (256 // d_seq, d_seq)
    wa = w_branch_a[l].astype(BF16)
    wb = w_branch_b[l].astype(BF16)
    wo = w_out[l].astype(BF16)
    rw = jnp.pad(router_w[l], ((0, 0), (0, LANES - router_w.shape[-1])))

    def mixer(x, mod4, tiles, post_tiles, attend, s0, buf0, chunk):
        a_q, a_k, a_v, g_qkv, g_z, gate_a, gate_b, small = in_proj(x, mod4, norm_mix[l], w_main, w_small, *tiles)
        y_a = attend(a_q, a_k, a_v)
        small_t = jnp.swapaxes(small[..., :2 * G_HEADS], 1, 2)
        y_b, s_new, tail = delta_rule(g_qkv, g_z, small, small_t, conv_w[l], a_log[l], dt_bias[l],
                                      delta_norm[l], s0, buf0, chunk)
        x1, h2, lg = post_mix(y_a, y_b, gate_a, gate_b, x, mod4, wa, wb, wo, norm_ffn[l], rw, *post_tiles)
        return x1, h2, lg, a_k, a_v, s_new, tail[:, SUBLANES - (CONV_W - 1):]

    s0_p = jnp.zeros((n_b, G_HEADS, HEAD_DIM, HEAD_DIM), F32)
    buf0_p = jnp.zeros((n_b, SUBLANES, conv_ch), F32)
    buf0_s = jnp.pad(state_conv[l], ((0, 0), (SUBLANES - (CONV_W - 1), 0), (0, 0)))

    x1_p, h2_p, lg_p, k_p, v_p, d_p, cv_p = mixer(
        x_prompt, mod_p, tiles_p, post_p, functools.partial(moba_prompt, rel_bias=rel_bias), s0_p, buf0_p,
        DELTA_CHUNK)
    x1_s, h2_s, lg_s, k_s, v_s, d_s, cv_s = mixer(
        x_sample, mod_s, tiles_s, post_s,
        functools.partial(moba_sample, cache_k=cache_k, cache_v=cache_v, layer=l, page_table=page_table,
                          rel_bias=rel_bias, past_len=past_len),
        state_delta[l], buf0_s, d_seq)

    comb_p = (1, 256)
    comb_s = (256 // d_seq, d_seq)
    y_p, y_s = _moe(h2_p, h2_s, x1_p, x1_s, lg_p, lg_s, mod_p, mod_s, router_bias[l],
                    w_gate_e[l], w_up_e[l], w_down_e[l],
                    w_gate_s[l].astype(BF16), w_up_s[l].astype(BF16), w_down_s[l].astype(BF16),
                    norm_final, comb_p, comb_s)

    def heads(t, n_heads):
        return t.reshape(t.shape[:-1] + (n_heads, HEAD_DIM))[None]

    return (y_p, y_s, heads(k_p, A_HEADS), heads(v_p, A_HEADS), heads(k_s, A_HEADS), heads(v_s, A_HEADS),
            d_p[None], cv_p[None], d_s[None], cv_s[None])
```

```python
import functools
import math

import numpy as np
import jax
import jax.numpy as jnp
from jax import lax
from jax.experimental import pallas as pl
from jax.experimental.pallas import tpu as pltpu

F32 = jnp.float32
BF16 = jnp.bfloat16
I32 = jnp.int32
EPS = 1e-6
NEG_INF = float("-inf")

A_HEADS = 8
G_HEADS = 8
MOBA_BLOCK = 256
MOBA_TOPK = 3
REL_BUCKETS = 32
REL_MAX_DIST = 128
CONV_W = 4
N_GROUPS = 8
TOPK_GROUPS = 4
TOP_K = 6
ROUTED_SCALE = 2.5
N_ADA = 6
HEAD_DIM = 128

LANES = 128
SUBLANES = 8
VMEM_LIMIT = 56 * 1024 * 1024
EXPERT_ROWS = 256
DELTA_CHUNK = 128
SAMPLE_BLOCKS = 8


def _cparams(sem, vmem=VMEM_LIMIT):
    return pltpu.CompilerParams(dimension_semantics=sem, vmem_limit_bytes=vmem)


def _dot(a, b):
    return jnp.dot(a.astype(BF16), b.astype(BF16), preferred_element_type=F32)


def _dot_nt(a, b):
    return lax.dot_general(a.astype(BF16), b.astype(BF16), (((1,), (1,)), ((), ())),
                           preferred_element_type=F32)


def _dot_tn(a, b):
    return lax.dot_general(a.astype(BF16), b.astype(BF16), (((0,), (0,)), ((), ())),
                           preferred_element_type=F32)


def _dot_exact(a, b):
    return jnp.dot(a, b, precision=lax.Precision.HIGHEST, preferred_element_type=F32)


_DIMS = {"nn": (((1,), (0,)), ((), ())), "nt": (((1,), (1,)), ((), ())), "tn": (((0,), (0,)), ((), ()))}

def _mm(a, b, form, passes):
    dims = _DIMS[form]
    a_hi = a.astype(BF16)
    b_hi = b.astype(BF16)
    out = lax.dot_general(a_hi, b_hi, dims, preferred_element_type=F32)
    if passes == 3:
        a_lo = (a - a_hi.astype(F32)).astype(BF16)
        b_lo = (b - b_hi.astype(F32)).astype(BF16)
        out = out + (lax.dot_general(a_hi, b_lo, dims, preferred_element_type=F32)
                     + lax.dot_general(a_lo, b_hi, dims, preferred_element_type=F32))
    return out


def _pack_pairs(x):
    w = x.shape[-1] // 2
    lo = lax.bitcast_convert_type(x[:, :w].astype(BF16).astype(F32), jnp.uint32)
    hi = lax.bitcast_convert_type(x[:, w:].astype(BF16).astype(F32), jnp.uint32)
    return (lo >> 16) | (hi & jnp.uint32(0xFFFF0000))


def _unpack_pairs(u):
    lo = lax.bitcast_convert_type(u << 16, F32)
    hi = lax.bitcast_convert_type(u & jnp.uint32(0xFFFF0000), F32)
    return lo, hi


def _silu(x):
    return x * jax.nn.sigmoid(x)


def _rel_bucket_np(dist):
    n = np.maximum(dist, 0)
    max_exact = REL_BUCKETS // 2
    nf = np.maximum(n, 1).astype(np.float32)
    large = max_exact + (np.log(nf / np.float32(max_exact)) / np.float32(math.log(REL_MAX_DIST / max_exact))
                         * np.float32(REL_BUCKETS - max_exact)).astype(np.int32)
    return np.where(n < max_exact, n, np.minimum(large, REL_BUCKETS - 1)).astype(np.int32)


def _ada_kernel(c_ref, w_ref, b_ref, o_ref):
    o_ref[...] = _dot(_silu(c_ref[...]), w_ref[...]) + b_ref[...]


def ada_mod(c_all, w_ada, b_ada):
    rows, d = c_all.shape
    n = w_ada.shape[1]
    tn = 1024
    return pl.pallas_call(
        _ada_kernel,
        out_shape=jax.ShapeDtypeStruct((rows, n), F32),
        grid=(n // tn,),
        in_specs=[pl.BlockSpec((rows, d), lambda j: (0, 0)),
                  pl.BlockSpec((d, tn), lambda j: (0, j)),
                  pl.BlockSpec((1, tn), lambda j: (0, j))],
        out_specs=pl.BlockSpec((rows, tn), lambda j: (0, j)),
        compiler_params=_cparams(("arbitrary",)),
        name="ada_mod",
    )(c_all, w_ada, b_ada.reshape(1, n))


IN_TILE = 1024
IN_GROUPS = (("a_q", 0, 1), ("a_k", 1, 1), ("a_v", 2, 1), ("g_qkv", 3, 3), ("g_z", 6, 1),
             ("gate_a", 7, 2), ("gate_b", 9, 2))
IN_COL_TILES = 11


def _inproj_kernel(x_ref, sc_ref, sh_ref, nw_ref, w_ref, ws_ref, *rest):
    out_refs = rest[:len(IN_GROUPS)]
    small_ref = rest[len(IN_GROUPS)]
    h_ref = rest[len(IN_GROUPS) + 1]
    bt, tl, d = x_ref.shape
    j = pl.program_id(2)

    @pl.when(j == 0)
    def _():
        x = x_ref[...]
        y = x * lax.rsqrt(jnp.mean(x * x, axis=-1, keepdims=True) + EPS) * nw_ref[...]
        h = y * (1.0 + sc_ref[:, 0]) + sh_ref[:, 0]
        h2 = h.reshape(bt * tl, d).astype(BF16)
        h_ref[...] = h2
        small_ref[...] = jnp.dot(h2, ws_ref[...], preferred_element_type=F32).reshape(bt, tl, LANES)

    res = jnp.dot(h_ref[...], w_ref[...], preferred_element_type=F32).reshape(bt, tl, IN_TILE)
    for o_ref, (_, j0, nj) in zip(out_refs, IN_GROUPS):
        @pl.when((j >= j0) & (j < j0 + nj))
        def _(o_ref=o_ref):
            o_ref[...] = res


def in_proj(x, mod4, norm_w, w_main, w_small, bt, tl):
    b, l, d = x.shape
    grid = (b // bt, l // tl, IN_COL_TILES)
    out_shapes, out_specs = [], []
    for _, j0, nj in IN_GROUPS:
        out_shapes.append(jax.ShapeDtypeStruct((b, l, nj * IN_TILE), F32))
        out_specs.append(pl.BlockSpec((bt, tl, IN_TILE),
                                      lambda i, t, j, j0=j0, nj=nj: (i, t, jnp.clip(j - j0, 0, nj - 1))))
    out_shapes.append(jax.ShapeDtypeStruct((b, l, LANES), F32))
    out_specs.append(pl.BlockSpec((bt, tl, LANES), lambda i, t, j: (i, t, 0)))
    return pl.pallas_call(
        _inproj_kernel,
        out_shape=out_shapes,
        grid=grid,
        in_specs=[pl.BlockSpec((bt, tl, d), lambda i, t, j: (i, t, 0)),
                  pl.BlockSpec((bt, 1, 1, d), lambda i, t, j: (i, 1, 0, 0)),
                  pl.BlockSpec((bt, 1, 1, d), lambda i, t, j: (i, 0, 0, 0)),
                  pl.BlockSpec((1, d), lambda i, t, j: (0, 0)),
                  pl.BlockSpec((d, IN_TILE), lambda i, t, j: (0, j)),
                  pl.BlockSpec((d, LANES), lambda i, t, j: (0, 0))],
        out_specs=out_specs,
        scratch_shapes=[pltpu.VMEM((bt * tl, d), BF16)],
        compiler_params=_cparams(("arbitrary", "arbitrary", "arbitrary")),
        name="in_proj",
    )(x, mod4, mod4, norm_w.reshape(1, d), w_main, w_small)


def _softmax_step(s, v_blk, m, l, acc):
    m_new = jnp.maximum(m, jnp.max(s, axis=-1, keepdims=True))
    alpha = jnp.exp(m - m_new)
    p = jnp.exp(s - m_new)
    l = alpha * l + jnp.sum(p, axis=-1, keepdims=True)
    acc = alpha * acc + _dot(p, v_blk)
    return m_new, l, acc


def _topk_select(cols, k):
    sels = []
    for n, gn in enumerate(cols):
        rank = jnp.zeros(gn.shape, F32)
        for m_, gm in enumerate(cols):
            if m_ == n:
                continue
            ahead = (gm >= gn) if m_ < n else (gm > gn)
            rank = rank + jnp.where(ahead, 1.0, 0.0)
        sels.append(rank < float(k))
    return sels


def _moba_prompt_kernel(rb_ref, bkt_ref, q_ref, k_ref, v_ref, o_ref, bias_ref):
    h = pl.program_id(0)
    s_len = q_ref.shape[1]
    blk = MOBA_BLOCK
    nb = s_len // blk
    scale = HEAD_DIM ** -0.5

    @pl.when(pl.program_id(1) == 0)
    def _():
        row = lax.broadcasted_iota(I32, (blk, blk), 0)
        col = lax.broadcasted_iota(I32, (blk, blk), 1)
        for t in range(2):
            bkt = bkt_ref[t]
            bias = jnp.zeros((blk, blk), F32)
            for r in range(REL_BUCKETS):
                bias = jnp.where(bkt == r, rb_ref[r, h], bias)
            bias_ref[t] = jnp.where(col <= row, bias, NEG_INF) if t == 0 else bias

    bias_far = rb_ref[REL_BUCKETS - 1, h]
    k_means = [jnp.mean(k_ref[0, n * blk:(n + 1) * blk, :], axis=0, keepdims=True) for n in range(nb)]

    for qb in range(nb):
        q = q_ref[0, qb * blk:(qb + 1) * blk, :]
        qs = (q * scale).astype(BF16)
        if qb > MOBA_TOPK:
            gates = [jnp.sum(q * k_means[n], axis=-1, keepdims=True) for n in range(qb)]
            masks = [jnp.where(sel, 0.0, NEG_INF) for sel in _topk_select(gates, MOBA_TOPK)]
        else:
            masks = [None] * qb
        s = _dot_nt(qs, k_ref[0, qb * blk:(qb + 1) * blk, :]) + bias_ref[0]
        m = jnp.max(s, axis=-1, keepdims=True)
        p = jnp.exp(s - m)
        l = jnp.sum(p, axis=-1, keepdims=True)
        acc = _dot(p, v_ref[0, qb * blk:(qb + 1) * blk, :])
        for n in range(qb - 1, -1, -1):
            s = _dot_nt(qs, k_ref[0, n * blk:(n + 1) * blk, :])
            if n == qb - 1:
                s = s + bias_ref[1]
                if masks[n] is not None:
                    s = s + masks[n]
            else:
                s = s + (bias_far if masks[n] is None else masks[n] + bias_far)
            m, l, acc = _softmax_step(s, v_ref[0, n * blk:(n + 1) * blk, :], m, l, acc)
        o_ref[0, qb * blk:(qb + 1) * blk, :] = acc / l


def moba_prompt(q, k, v, rel_bias):
    b, s_len, width = q.shape
    assert s_len % MOBA_BLOCK == 0 and width == A_HEADS * HEAD_DIM
    ar = np.arange(MOBA_BLOCK)
    d_loc = ar[:, None] - ar[None, :]
    bkt = np.stack([_rel_bucket_np(d_loc), _rel_bucket_np(d_loc + MOBA_BLOCK)]).astype(np.int32)
    assert int(_rel_bucket_np(np.array([MOBA_BLOCK + 1]))[0]) == REL_BUCKETS - 1
    spec = pl.BlockSpec((1, s_len, HEAD_DIM), lambda h, i: (i, 0, h))
    return pl.pallas_call(
        _moba_prompt_kernel,
        out_shape=jax.ShapeDtypeStruct((b, s_len, width), F32),
        grid=(A_HEADS, b),
        in_specs=[pl.BlockSpec(memory_space=pltpu.SMEM),
                  pl.BlockSpec((2, MOBA_BLOCK, MOBA_BLOCK), lambda h, i: (0, 0, 0)),
                  spec, spec, spec],
        out_specs=spec,
        scratch_shapes=[pltpu.VMEM((2, MOBA_BLOCK, MOBA_BLOCK), F32)],
        compiler_params=_cparams(("arbitrary", "arbitrary")),
        name="moba_prompt",
    )(rel_bias, jnp.asarray(bkt), q, k, v)


def _moba_sample_kernel(pt_ref, rb_ref, bkt_ref, q_ref, kn_ref, vn_ref, *rest, past_len):
    nh = A_HEADS
    n_pg = 2 * SAMPLE_BLOCKS
    k_pages_all = rest[0:n_pg]
    v_pages_all = rest[n_pg:2 * n_pg]
    o_ref, bias_ref, m_ref, l_ref, g_ref, acc_ref = rest[2 * n_pg:]
    page = k_pages_all[0].shape[1] // nh

    def head_rows(ref, h):
        return ref[0, pl.ds(h, page, stride=nh), :]

    n = pl.program_id(1)
    nb = pl.num_programs(1)
    t = q_ref.shape[1]
    dh = HEAD_DIM
    blk = MOBA_BLOCK
    scale = dh ** -0.5
    n_last = past_len // blk - 1

    @pl.when(n == 0)
    def _():
        for h in range(nh):
            bkt = bkt_ref[...]
            bias = jnp.zeros((t, blk), F32)
            for r in range(REL_BUCKETS):
                bias = jnp.where(bkt == r, rb_ref[r, h], bias)
            bias_ref[h * t:(h + 1) * t, :] = bias

    q_all = q_ref[0]
    qhs = [q_all[:, h * dh:(h + 1) * dh] for h in range(nh)]
    qss = [(qh * scale).astype(BF16) for qh in qhs]
    heads = range(nh)
    for jb in range(SAMPLE_BLOCKS):
        gi = n * SAMPLE_BLOCKS + jb
        k_pages = k_pages_all[2 * jb:2 * jb + 2]
        v_pages = v_pages_all[2 * jb:2 * jb + 2]
        k_blks = [jnp.concatenate([head_rows(k_pages[0], h), head_rows(k_pages[1], h)], axis=0) for h in heads]
        v_blks = [jnp.concatenate([head_rows(v_pages[0], h), head_rows(v_pages[1], h)], axis=0) for h in heads]
        ss = [_dot_nt(qss[h], k_blks[h]) for h in heads]
        gs = [jnp.sum(qhs[h] * jnp.mean(k_blks[h], axis=0, keepdims=True), axis=-1, keepdims=True)
              for h in heads]
        ss = [ss[h] + jnp.where(gi == n_last, bias_ref[h * t:(h + 1) * t, :], rb_ref[REL_BUCKETS - 1, h])
              for h in heads]
        ms = [jnp.max(s, axis=-1, keepdims=True) for s in ss]
        ps = [jnp.exp(s - m) for s, m in zip(ss, ms)]
        accs = [_dot(ps[h], v_blks[h]) for h in heads]
        m_ref[gi] = jnp.concatenate(ms, axis=0)
        l_ref[gi] = jnp.concatenate([jnp.sum(p, axis=-1, keepdims=True) for p in ps], axis=0)
        g_ref[gi] = jnp.concatenate(gs, axis=0)
        acc_ref[gi] = jnp.concatenate(accs, axis=0)

    @pl.when(n == nb - 1)
    def _():
        n_blocks = past_len // blk
        rows = nh * t
        sels = _topk_select([g_ref[i] for i in range(n_blocks)], MOBA_TOPK)
        tq = lax.broadcasted_iota(I32, (t, t), 0)
        tk = lax.broadcasted_iota(I32, (t, t), 1)
        s_locs = []
        for h in range(nh):
            bias_loc = jnp.zeros((t, t), F32)
            for d in range(t):
                bias_loc = jnp.where(tq - tk == d, rb_ref[d, h], bias_loc)
            s_loc = _dot_nt(qss[h], kn_ref[0, :, h * dh:(h + 1) * dh]) + bias_loc
            s_locs.append(jnp.where(tk <= tq, s_loc, NEG_INF))
        s_loc = jnp.concatenate(s_locs, axis=0)
        m_tot = jnp.max(s_loc, axis=-1, keepdims=True)
        for i in range(n_blocks):
            m_tot = jnp.maximum(m_tot, jnp.where(sels[i], m_ref[i], NEG_INF))
        p_loc = jnp.exp(s_loc - m_tot)
        l_tot = jnp.sum(p_loc, axis=-1, keepdims=True)
        acc = jnp.concatenate([_dot(p_loc[h * t:(h + 1) * t, :], vn_ref[0, :, h * dh:(h + 1) * dh])
                               for h in range(nh)], axis=0)
        for i in range(n_blocks):
            w = jnp.where(sels[i], jnp.exp(jnp.where(sels[i], m_ref[i] - m_tot, 0.0)), 0.0)
            l_tot = l_tot + w * l_ref[i]
            acc = acc + w * acc_ref[i]
        out = acc / l_tot
        o_ref[0] = jnp.concatenate([out[h * t:(h + 1) * t, :] for h in range(nh)], axis=1)


def moba_sample(q, k_new, v_new, cache_k, cache_v, layer, page_table, rel_bias, past_len):
    db, t, width = q.shape
    page = cache_k.shape[2]
    assert MOBA_BLOCK == 2 * page and past_len % MOBA_BLOCK == 0 and t <= REL_BUCKETS // 2
    assert past_len // MOBA_BLOCK >= MOBA_TOPK and t % SUBLANES == 0
    n_blocks = past_len // MOBA_BLOCK
    assert n_blocks % SAMPLE_BLOCKS == 0
    rows = A_HEADS * t
    d_last = MOBA_BLOCK + np.arange(t)[:, None] - np.arange(MOBA_BLOCK)[None, :]
    bkt = _rel_bucket_np(d_last).astype(np.int32)
    assert int(_rel_bucket_np(np.array([MOBA_BLOCK + 1]))[0]) == REL_BUCKETS - 1
    tok_spec = pl.BlockSpec((1, t, width), lambda i, n, pt: (i, 0, 0))

    n_phys = cache_k.shape[1]
    ck = cache_k.reshape(cache_k.shape[0] * n_phys, page * A_HEADS, HEAD_DIM)
    cv = cache_v.reshape(cache_v.shape[0] * n_phys, page * A_HEADS, HEAD_DIM)

    n_pg = 2 * SAMPLE_BLOCKS
    page_specs = [pl.BlockSpec((1, page * A_HEADS, HEAD_DIM),
                               lambda i, n, pt, j=j: (layer * n_phys + pt[i, n_pg * n + j], 0, 0))
                  for j in range(n_pg)]

    grid_spec = pltpu.PrefetchScalarGridSpec(
        num_scalar_prefetch=1,
        grid=(db, n_blocks // SAMPLE_BLOCKS),
        in_specs=[pl.BlockSpec(memory_space=pltpu.SMEM),
                  pl.BlockSpec((t, MOBA_BLOCK), lambda i, n, pt: (0, 0)),
                  tok_spec, tok_spec, tok_spec] + page_specs + page_specs,
        out_specs=tok_spec,
        scratch_shapes=[pltpu.VMEM((rows, MOBA_BLOCK), F32),
                        pltpu.VMEM((n_blocks, rows, 1), F32),
                        pltpu.VMEM((n_blocks, rows, 1), F32),
                        pltpu.VMEM((n_blocks, rows, 1), F32),
                        pltpu.VMEM((n_blocks, rows, HEAD_DIM), F32)],
    )
    return pl.pallas_call(
        functools.partial(_moba_sample_kernel, past_len=past_len),
        out_shape=jax.ShapeDtypeStruct((db, t, width), F32),
        grid_spec=grid_spec,
        compiler_params=_cparams(("arbitrary", "arbitrary")),
        name="moba_sample",
    )(page_table, rel_bias, jnp.asarray(bkt), q, k_new, v_new, *([ck] * n_pg + [cv] * n_pg))


INV_BASE = 16


def _unit_lower_inverses(a_mats, ri, ci, size):
    base = min(INV_BASE, size)
    same = ri // base == ci // base
    eye = jnp.where(ri == ci, 1.0, 0.0)
    pws = [jnp.where(same, a, 0.0) for a in a_mats]
    invs = [eye - p for p in pws]
    for _ in range(max(int(math.log2(base)) - 1, 0)):
        pws = [_mm(p, p, "nn", 1) for p in pws]
        invs = [i + _mm(i, p, "nn", 1) for i, p in zip(invs, pws)]
    blk = base
    while blk < size:
        pr = ri // blk
        pc = ci // blk
        join = (pr == pc + 1) & (pr // 2 == pc // 2)
        halves = [_mm(i, jnp.where(join, a, 0.0), "nn", 1) for i, a in zip(invs, a_mats)]
        invs = [i - _mm(hf, i, "nn", 1) for i, hf in zip(invs, halves)]
        blk *= 2
    return invs


def _delta_kernel(x_ref, cw_ref, z_ref, sm_ref, smt_ref, ar_ref, dr_ref, ac_ref, dc_ref, nw_ref, s0_ref, b0_ref,
                  y_ref, sn_ref, tail_ref, s_sc, carry_sc):
    c = pl.program_id(1)
    nc = pl.num_programs(1)
    chunk = x_ref.shape[1]
    dk = HEAD_DIM
    hh = G_HEADS
    part = hh * dk

    @pl.when(c == 0)
    def _():
        s_sc[...] = s0_ref[0]
        carry_sc[...] = b0_ref[0]

    x = x_ref[0]
    xx = jnp.concatenate([carry_sc[...], x], axis=0)
    cw = cw_ref[...]
    y = x * cw[CONV_W - 1:CONV_W, :]
    for s in range(1, CONV_W):
        y = y + xx[SUBLANES - s:SUBLANES - s + chunk, :] * cw[CONV_W - 1 - s:CONV_W - s, :]
    carry_sc[...] = xx[chunk:chunk + SUBLANES, :]
    qkv = _silu(y)

    def softplus(t):
        return jnp.maximum(t, 0.0) + jnp.log(1.0 + jnp.exp(-jnp.abs(t)))

    sm = sm_ref[0]
    beta_all = jax.nn.sigmoid(sm)
    g_cols = -jnp.exp(ar_ref[...]) * softplus(sm + dr_ref[...])
    g_rows = -jnp.exp(ac_ref[...]) * softplus(smt_ref[0] + dc_ref[...])
    ri = lax.broadcasted_iota(I32, (chunk, chunk), 0)
    ci = lax.broadcasted_iota(I32, (chunk, chunk), 1)
    incl = ci <= ri
    strict = ci < ri
    gc_cols = _dot_exact(jnp.where(incl, 1.0, 0.0), g_cols)
    gc_rows = _dot_exact(g_rows, jnp.where(ri <= ci, 1.0, 0.0))
    nw = nw_ref[...]

    heads = range(hh)
    qs, ks, vs, betas, gcs, dmasks = [], [], [], [], [], []
    for h in heads:
        q = qkv[:, h * dk:(h + 1) * dk]
        k = qkv[:, part + h * dk:part + (h + 1) * dk]
        qs.append(q * lax.rsqrt(jnp.sum(q * q, axis=-1, keepdims=True) + EPS) * (dk ** -0.5))
        ks.append(k * lax.rsqrt(jnp.sum(k * k, axis=-1, keepdims=True) + EPS))
        vs.append(qkv[:, 2 * part + h * dk:2 * part + (h + 1) * dk])
        betas.append(beta_all[:, h:h + 1])
        gc_c = gc_cols[:, hh + h:hh + h + 1]
        gc_r = gc_rows[hh + h:hh + h + 1, :]
        gcs.append(gc_c)
        dmasks.append(jnp.exp(jnp.where(incl, gc_c - gc_r, NEG_INF)))

    kbs = [k * b for k, b in zip(ks, betas)]
    a_mats = [jnp.where(strict, _mm(kb, k, "nt", 1) * dm, 0.0) for kb, k, dm in zip(kbs, ks, dmasks)]
    t_invs = _unit_lower_inverses(a_mats, ri, ci, chunk)
    egs = [jnp.exp(g) for g in gcs]
    uws = [_mm(t, jnp.concatenate([v * b, kb * eg], axis=1), "nn", 1)
           for t, v, b, kb, eg in zip(t_invs, vs, betas, kbs, egs)]
    s_mats = [s_sc[h] for h in heads]
    v_news = [uw[:, :dk] - _mm(uw[:, dk:], s, "nn", 1) for uw, s in zip(uws, s_mats)]
    attns = [_mm(q, k, "nt", 1) * dm for q, k, dm in zip(qs, ks, dmasks)]
    outs = [_mm(q * eg, s, "nn", 1) + _mm(at, vn, "nn", 1)
            for q, eg, s, at, vn in zip(qs, egs, s_mats, attns, v_news)]
    for h in heads:
        g_last = gcs[h][chunk - 1:chunk, :]
        s_sc[h] = s_mats[h] * jnp.exp(g_last) + _mm(ks[h] * jnp.exp(g_last - gcs[h]), v_news[h], "tn", 1)
    for h in heads:
        o = outs[h]
        o_n = o * lax.rsqrt(jnp.mean(o * o, axis=-1, keepdims=True) + EPS) * nw
        y_ref[0, :, h * dk:(h + 1) * dk] = o_n * _silu(z_ref[0, :, h * dk:(h + 1) * dk])

    @pl.when(c == nc - 1)
    def _():
        sn_ref[0] = s_sc[...]
        tail_ref[0] = carry_sc[...]


def delta_rule(xg, z, small, small_t, conv_w, a_log, dt_bias, delta_norm, s0, buf0, chunk):
    b, l, cw_ch = xg.shape
    hh = G_HEADS
    assert l % chunk == 0 and cw_ch == 3 * hh * HEAD_DIM and l >= CONV_W - 1
    nc = l // chunk
    width = hh * HEAD_DIM
    a_row = jnp.pad(a_log.reshape(1, hh), ((0, 0), (hh, LANES - 2 * hh)))
    d_row = jnp.pad(dt_bias.reshape(1, hh), ((0, 0), (hh, LANES - 2 * hh)))
    a_col = jnp.pad(a_log.reshape(hh, 1), ((hh, 0), (0, 0)))
    d_col = jnp.pad(dt_bias.reshape(hh, 1), ((hh, 0), (0, 0)))

    def const(shape):
        return pl.BlockSpec(shape, lambda i, c: (0,) * len(shape))

    return pl.pallas_call(
        _delta_kernel,
        out_shape=[jax.ShapeDtypeStruct((b, l, width), F32),
                   jax.ShapeDtypeStruct((b, hh, HEAD_DIM, HEAD_DIM), F32),
                   jax.ShapeDtypeStruct((b, SUBLANES, cw_ch), F32)],
        grid=(b, nc),
        in_specs=[pl.BlockSpec((1, chunk, cw_ch), lambda i, c: (i, c, 0)),
                  const((CONV_W, cw_ch)),
                  pl.BlockSpec((1, chunk, width), lambda i, c: (i, c, 0)),
                  pl.BlockSpec((1, chunk, LANES), lambda i, c: (i, c, 0)),
                  pl.BlockSpec((1, 2 * hh, chunk), lambda i, c: (i, 0, c)),
                  const((1, LANES)), const((1, LANES)), const((2 * hh, 1)), const((2 * hh, 1)),
                  const((1, HEAD_DIM)),
                  pl.BlockSpec((1, hh, HEAD_DIM, HEAD_DIM), lambda i, c: (i, 0, 0, 0)),
                  pl.BlockSpec((1, SUBLANES, cw_ch), lambda i, c: (i, 0, 0))],
        out_specs=[pl.BlockSpec((1, chunk, width), lambda i, c: (i, c, 0)),
                   pl.BlockSpec((1, hh, HEAD_DIM, HEAD_DIM), lambda i, c: (i, 0, 0, 0)),
                   pl.BlockSpec((1, SUBLANES, cw_ch), lambda i, c: (i, 0, 0))],
        scratch_shapes=[pltpu.VMEM((hh, HEAD_DIM, HEAD_DIM), F32),
                        pltpu.VMEM((SUBLANES, cw_ch), F32)],
        compiler_params=_cparams(("arbitrary", "arbitrary")),
        name="delta_rule",
    )(xg, conv_w, z, small, small_t, a_row, d_row, a_col, d_col, delta_norm.reshape(1, HEAD_DIM), s0, buf0)


def _post_kernel(ya_ref, yb_ref, ga_ref, gb_ref, x_ref, gm_ref, scf_ref, shf_ref, wa_ref, wb_ref, wo_ref,
                 nw_ref, rw_ref, x1_ref, h2_ref, lg_ref):
    bt, tl, d = x_ref.shape
    rows = bt * tl
    ya = ya_ref[...].reshape(rows, -1)
    yb = yb_ref[...].reshape(rows, -1)
    merged = (jax.nn.sigmoid(ga_ref[...].reshape(rows, d)) * _dot(ya, wa_ref[...])
              + jax.nn.sigmoid(gb_ref[...].reshape(rows, d)) * _dot(yb, wb_ref[...]))
    mix = _dot(merged, wo_ref[...]).reshape(bt, tl, d)
    x1 = x_ref[...] + gm_ref[:, 0] * mix
    x1_ref[...] = x1
    y = x1 * lax.rsqrt(jnp.mean(x1 * x1, axis=-1, keepdims=True) + EPS) * nw_ref[...]
    h2 = (y * (1.0 + scf_ref[:, 0]) + shf_ref[:, 0]).reshape(rows, d)
    h2_ref[...] = _pack_pairs(h2).reshape(bt, tl, d // 2)
    lg_ref[...] = _mm(h2, rw_ref[...], "nn", 3).reshape(bt, tl, -1)


def post_mix(y_a, y_b, gate_a, gate_b, x, mod4, w_a, w_b, w_o, norm_w, router_w, bt, tl):
    b, l, d = x.shape
    wa_in = y_a.shape[-1]
    wb_in = y_b.shape[-1]
    n_e = router_w.shape[1]

    def tok(width):
        return pl.BlockSpec((bt, tl, width), lambda i, t: (i, t, 0))

    def modspec(idx):
        return pl.BlockSpec((bt, 1, 1, d), lambda i, t, idx=idx: (i, idx, 0, 0))

    def const(shape):
        return pl.BlockSpec(shape, lambda i, t: (0, 0), pipeline_mode=pl.Buffered(1))

    return pl.pallas_call(
        _post_kernel,
        out_shape=[jax.ShapeDtypeStruct((b, l, d), F32), jax.ShapeDtypeStruct((b, l, d // 2), jnp.uint32),
                   jax.ShapeDtypeStruct((b, l, n_e), F32)],
        grid=(b // bt, l // tl),
        in_specs=[tok(wa_in), tok(wb_in), tok(d), tok(d), tok(d), modspec(2), modspec(4), modspec(3),
                  const((wa_in, d)), const((wb_in, d)), const((d, d)), const((1, d)), const((d, n_e))],
        out_specs=[tok(d), tok(d // 2), tok(n_e)],
        compiler_params=_cparams(("arbitrary", "arbitrary")),
        name="post_mix",
    )(y_a, y_b, gate_a, gate_b, x, mod4, mod4, mod4, w_a, w_b, w_o, norm_w.reshape(1, d), router_w)


def _route_kernel(lg_ref, rb_ref, e_ref, w_ref, p_ref, cnt_ref, run_sc, *, n_e):
    i = pl.program_id(0)
    tm, width = lg_ref.shape
    per_group = n_e // N_GROUPS

    @pl.when(i == 0)
    def _():
        run_sc[...] = jnp.zeros_like(run_sc)

    lane = lax.broadcasted_iota(I32, (tm, width), 1)
    scores = jnp.where(lane < n_e, jax.nn.sigmoid(lg_ref[...]), 0.0)
    biased = jnp.where(lane < n_e, scores + rb_ref[...], NEG_INF)
    grp = lane // per_group

    def first_argmax(vals):
        mx = jnp.max(vals, axis=-1, keepdims=True)
        idx = jnp.min(jnp.where(vals == mx, lane, width), axis=-1, keepdims=True)
        return mx, idx

    group_scores = []
    for g in range(N_GROUPS):
        mg = jnp.where(grp == g, biased, NEG_INF)
        m1, i1 = first_argmax(mg)
        m2 = jnp.max(jnp.where(lane == i1, NEG_INF, mg), axis=-1, keepdims=True)
        group_scores.append(m1 + m2)
    keep = _topk_select(group_scores, TOPK_GROUPS)
    expert_keep = jnp.zeros((tm, width), jnp.bool_)
    for g in range(N_GROUPS):
        expert_keep = expert_keep | ((grp == g) & keep[g])
    masked = jnp.where(expert_keep, biased, NEG_INF)

    sel = jnp.zeros((tm, width), jnp.bool_)
    idxs = []
    for _ in range(TOP_K):
        _, idx = first_argmax(masked)
        hit = lane == idx
        sel = sel | hit
        masked = jnp.where(hit, NEG_INF, masked)
        idxs.append(idx)
    sel_f = jnp.where(sel, 1.0, 0.0)
    top_w = scores * sel_f
    top_w = top_w / jnp.sum(top_w, axis=-1, keepdims=True) * ROUTED_SCALE

    ri = lax.broadcasted_iota(I32, (tm, tm), 0)
    ci = lax.broadcasted_iota(I32, (tm, tm), 1)
    before = _dot(jnp.where(ci < ri, 1.0, 0.0), sel_f) + run_sc[...]
    run_sc[...] = run_sc[...] + jnp.sum(sel_f, axis=0, keepdims=True)
    cnt_ref[...] = run_sc[...].astype(I32)

    lane_o = lax.broadcasted_iota(I32, (tm, LANES), 1)
    e_out = jnp.zeros((tm, LANES), I32)
    w_out = jnp.zeros((tm, LANES), F32)
    p_out = jnp.zeros((tm, LANES), I32)
    for kk, idx in enumerate(idxs):
        hit = lane == idx
        wk = jnp.sum(jnp.where(hit, top_w, 0.0), axis=-1, keepdims=True)
        pk = jnp.sum(jnp.where(hit, before, 0.0), axis=-1, keepdims=True).astype(I32)
        e_out = jnp.where(lane_o == kk, idx, e_out)
        w_out = jnp.where(lane_o == kk, wk, w_out)
        p_out = jnp.where(lane_o == kk, pk, p_out)
    e_ref[...] = e_out
    w_ref[...] = w_out
    p_ref[...] = p_out


def route(logits, router_bias, n_e, tm):
    n, width = logits.shape
    assert n % tm == 0 and width == LANES and n_e <= LANES
    tok = pl.BlockSpec((tm, LANES), lambda i: (i, 0))
    one = pl.BlockSpec((1, LANES), lambda i: (0, 0))
    bias = jnp.pad(router_bias.reshape(1, n_e), ((0, 0), (0, LANES - n_e)))
    return pl.pallas_call(
        functools.partial(_route_kernel, n_e=n_e),
        out_shape=[jax.ShapeDtypeStruct((n, LANES), I32), jax.ShapeDtypeStruct((n, LANES), F32),
                   jax.ShapeDtypeStruct((n, LANES), I32), jax.ShapeDtypeStruct((1, LANES), I32)],
        grid=(n // tm,),
        in_specs=[tok, one],
        out_specs=[tok, tok, tok, one],
        scratch_shapes=[pltpu.VMEM((1, LANES), F32)],
        compiler_params=_cparams(("arbitrary",)),
        name="route",
    )(logits, bias)


def _slot_rows_kernel(ps_ref, e_ref, p_ref, o_ref, *, n_e):
    e = e_ref[...]
    base = jnp.zeros(e.shape, I32)
    for x in range(n_e):
        base = jnp.where(e == x, ps_ref[x], base)
    o_ref[...] = base + p_ref[...]


def slot_rows(sel_e, sel_pos, pad_starts, tm):
    n = sel_e.shape[0]
    n_e = pad_starts.shape[0]
    tok = pl.BlockSpec((tm, LANES), lambda i: (i, 0))
    return pl.pallas_call(
        functools.partial(_slot_rows_kernel, n_e=n_e),
        out_shape=jax.ShapeDtypeStruct((n, LANES), I32),
        grid=(n // tm,),
        in_specs=[pl.BlockSpec(memory_space=pltpu.SMEM), tok, tok],
        out_specs=tok,
        compiler_params=_cparams(("arbitrary",)),
        name="slot_rows",
    )(pad_starts.astype(I32), sel_e, sel_pos)


def _row_copy(src_ref, dst_ref, src_row, dst_row, sem):
    return pltpu.make_async_copy(src_ref.at[pl.ds(src_row, 1)], dst_ref.at[pl.ds(dst_row, 1)], sem)


def _dispatch_kernel(dest_ref, h_ref, hs_ref, sem):
    tm = h_ref.shape[0]

    def body(t, carry):
        for kk in range(TOP_K):
            _row_copy(h_ref, hs_ref, t, dest_ref[0, 0, t * TOP_K + kk], sem).start()
        return carry

    def drain(t, carry):
        for kk in range(TOP_K):
            _row_copy(h_ref, hs_ref, t, dest_ref[0, 0, t * TOP_K + kk], sem).wait()
        return carry

    lax.fori_loop(0, tm, body, 0, unroll=4)
    lax.fori_loop(0, tm, drain, 0, unroll=4)


def dispatch(h_all, dest, n_rows, tm):
    n, d = h_all.shape
    assert n % tm == 0
    dest3 = dest.reshape(n // tm, 1, tm * TOP_K)
    return pl.pallas_call(
        _dispatch_kernel,
        out_shape=jax.ShapeDtypeStruct((n_rows, d), h_all.dtype),
        grid=(n // tm,),
        in_specs=[pl.BlockSpec((1, 1, tm * TOP_K), lambda i: (i, 0, 0), memory_space=pltpu.SMEM),
                  pl.BlockSpec((tm, d), lambda i: (i, 0))],
        out_specs=pl.BlockSpec(memory_space=pl.ANY),
        scratch_shapes=[pltpu.SemaphoreType.DMA(())],
        compiler_params=_cparams(("arbitrary",)),
        name="dispatch",
    )(dest3, h_all)


def _experts_kernel(be_ref, bf_ref, bv_ref, x_ref, wg_ref, wu_ref, wd_ref, y_ref, wg_sc, wu_sc, wd_sc):
    i = pl.program_id(0)
    valid = bv_ref[i]

    @pl.when(bf_ref[i] == 1)
    def _():
        wg_sc[...] = wg_ref[0].astype(BF16)
        wu_sc[...] = wu_ref[0].astype(BF16)
        wd_sc[...] = wd_ref[0].astype(BF16)

    @pl.when(valid > 0)
    def _():
        row = lax.broadcasted_iota(I32, x_ref.shape, 0)
        x = jnp.where(row < valid, x_ref[...], jnp.uint32(0))
        y_ref[...] = _pack_pairs(_swiglu_packed(x, wg_sc, wu_sc, wd_sc))

    @pl.when(valid == 0)
    def _():
        y_ref[...] = jnp.zeros_like(y_ref)


def _swiglu_packed(xp, wg_ref, wu_ref, wd_ref):
    lo, hi = _unpack_pairs(xp)
    lo = lo.astype(BF16)
    hi = hi.astype(BF16)
    half = xp.shape[-1]

    def proj(w_ref):
        return (jnp.dot(lo, w_ref[:half, :], preferred_element_type=F32)
                + jnp.dot(hi, w_ref[half:, :], preferred_element_type=F32))

    act = (_silu(proj(wg_ref)) * proj(wu_ref)).astype(BF16)
    return jnp.dot(act, wd_ref[...], preferred_element_type=F32)


def experts(h_sorted, blk_e, blk_first, blk_valid, w_gate, w_up, w_down):
    n_rows, dp = h_sorted.shape
    n_blocks = n_rows // EXPERT_ROWS
    d, de = w_gate.shape[-2:]
    assert d == 2 * dp
    grid_spec = pltpu.PrefetchScalarGridSpec(
        num_scalar_prefetch=3,
        grid=(n_blocks,),
        in_specs=[pl.BlockSpec((EXPERT_ROWS, dp), lambda i, be, bf, nu: (i, 0)),
                  pl.BlockSpec((1, d, de), lambda i, be, bf, nu: (be[i], 0, 0)),
                  pl.BlockSpec((1, d, de), lambda i, be, bf, nu: (be[i], 0, 0)),
                  pl.BlockSpec((1, de, d), lambda i, be, bf, nu: (be[i], 0, 0))],
        out_specs=pl.BlockSpec((EXPERT_ROWS, dp), lambda i, be, bf, nu: (i, 0)),
        scratch_shapes=[pltpu.VMEM((d, de), BF16), pltpu.VMEM((d, de), BF16), pltpu.VMEM((de, d), BF16)],
    )
    return pl.pallas_call(
        _experts_kernel,
        out_shape=jax.ShapeDtypeStruct((n_rows, dp), jnp.uint32),
        grid_spec=grid_spec,
        compiler_params=_cparams(("arbitrary",)),
        name="experts",
    )(blk_e, blk_first, blk_valid, h_sorted, w_gate, w_up, w_down)


def _combine_kernel(dest_ref, x1_ref, h2_ref, gf_ref, w_ref, wg_ref, wu_ref, wd_ref, nw_ref, yr_ref, o_ref,
                    gbuf, sem):
    bt, tl, d = x1_ref.shape
    tm = bt * tl

    def body(t, carry):
        for kk in range(TOP_K):
            _row_copy(yr_ref, gbuf.at[kk], dest_ref[0, 0, t * TOP_K + kk], t, sem).start()
        return carry

    lax.fori_loop(0, tm, body, 0, unroll=4)
    y = _swiglu_packed(h2_ref[...].reshape(tm, d // 2), wg_ref, wu_ref, wd_ref)
    wts = w_ref[...]

    def drain(t, carry):
        for kk in range(TOP_K):
            _row_copy(yr_ref, gbuf.at[kk], dest_ref[0, 0, t * TOP_K + kk], t, sem).wait()
        return carry

    lax.fori_loop(0, tm, drain, 0)
    r_lo = jnp.zeros((tm, d // 2), F32)
    r_hi = jnp.zeros((tm, d // 2), F32)
    for kk in range(TOP_K):
        lo, hi = _unpack_pairs(gbuf[kk])
        r_lo = r_lo + lo * wts[:, kk:kk + 1]
        r_hi = r_hi + hi * wts[:, kk:kk + 1]
    y = y + jnp.concatenate([r_lo, r_hi], axis=1)
    x2 = x1_ref[...] + gf_ref[:, 0] * y.reshape(bt, tl, d)
    o_ref[...] = x2 * lax.rsqrt(jnp.mean(x2 * x2, axis=-1, keepdims=True) + EPS) * nw_ref[...]


def combine(x1, h2, mod4, sel_w, dest, y_rows, w_gs, w_us, w_ds, norm_final, bt, tl):
    b, l, d = x1.shape
    tm = bt * tl
    n = b * l
    ds = w_gs.shape[1]
    n_t = l // tl
    dest3 = dest.reshape(n // tm, 1, tm * TOP_K)

    def const(shape):
        return pl.BlockSpec(shape, lambda i, t: (0, 0), pipeline_mode=pl.Buffered(1))

    return pl.pallas_call(
        _combine_kernel,
        out_shape=jax.ShapeDtypeStruct((b, l, d), F32),
        grid=(b // bt, n_t),
        in_specs=[pl.BlockSpec((1, 1, tm * TOP_K), lambda i, t: (i * n_t + t, 0, 0), memory_space=pltpu.SMEM),
                  pl.BlockSpec((bt, tl, d), lambda i, t: (i, t, 0)),
                  pl.BlockSpec((bt, tl, d // 2), lambda i, t: (i, t, 0)),
                  pl.BlockSpec((bt, 1, 1, d), lambda i, t: (i, 5, 0, 0)),
                  pl.BlockSpec((tm, LANES), lambda i, t: (i * n_t + t, 0)),
                  const((d, ds)), const((d, ds)), const((ds, d)), const((1, d)),
                  pl.BlockSpec(memory_space=pl.ANY)],
        out_specs=pl.BlockSpec((bt, tl, d), lambda i, t: (i, t, 0)),
        scratch_shapes=[pltpu.VMEM((TOP_K, tm, d // 2), jnp.uint32), pltpu.SemaphoreType.DMA(())],
        compiler_params=_cparams(("arbitrary", "arbitrary")),
        name="combine",
    )(dest3, x1, h2, mod4, sel_w, w_gs, w_us, w_ds, norm_final.reshape(1, d), y_rows)


def _moe(h2_p, h2_s, x1_p, x1_s, lg_p, lg_s, mod_p, mod_s, router_bias, w_gate_e, w_up_e, w_down_e,
         w_gs, w_us, w_ds, norm_final, tiles_p, tiles_s):
    d = h2_p.shape[-1]
    n_p = h2_p.shape[0] * h2_p.shape[1]
    n_s = h2_s.shape[0] * h2_s.shape[1]
    n = n_p + n_s
    n_e = w_gate_e.shape[0]
    h_all = jnp.concatenate([h2_p.reshape(n_p, d), h2_s.reshape(n_s, d)], axis=0)
    logits = jnp.concatenate([lg_p.reshape(n_p, LANES), lg_s.reshape(n_s, LANES)], axis=0)
    sel_e, sel_w, sel_pos, counts = route(logits, router_bias, n_e, 512)
    counts = counts[0, :n_e]
    padded = (counts + EXPERT_ROWS - 1) // EXPERT_ROWS * EXPERT_ROWS
    pad_ends = jnp.cumsum(padded)
    pad_starts = pad_ends - padded
    n_blocks = -(-(n * TOP_K + n_e * (EXPERT_ROWS - 1)) // EXPERT_ROWS)
    n_rows = n_blocks * EXPERT_ROWS
    dest = slot_rows(sel_e, sel_pos, pad_starts, 512)[:, :TOP_K]
    blk_start = jnp.arange(n_blocks, dtype=I32) * EXPERT_ROWS
    n_used = (pad_ends[-1] // EXPERT_ROWS).astype(I32)
    blk_e = jnp.minimum(jnp.sum((blk_start[:, None] >= pad_ends[None, :]).astype(I32), axis=1), n_e - 1)
    blk_valid = jnp.clip((pad_starts + counts)[blk_e] - blk_start, 0, EXPERT_ROWS).astype(I32)
    last_used_e = blk_e[jnp.maximum(n_used - 1, 0)]
    blk_e = jnp.where(jnp.arange(n_blocks) < n_used, blk_e, last_used_e)
    blk_first = jnp.concatenate([jnp.ones((1,), I32), (blk_e[1:] != blk_e[:-1]).astype(I32)])
    h_sorted = dispatch(h_all, dest, n_rows, 512)
    y_rows = experts(h_sorted, blk_e, blk_first, blk_valid, w_gate_e, w_up_e, w_down_e)
    y_p = combine(x1_p, h2_p, mod_p, sel_w[:n_p], dest[:n_p], y_rows, w_gs, w_us, w_ds, norm_final, *tiles_p)
    y_s = combine(x1_s, h2_s, mod_s, sel_w[n_p:], dest[n_p:], y_rows, w_gs, w_us, w_ds, norm_final, *tiles_s)
    return y_p, y_s


def kernel(x_prompt, x_sample, cache_k, cache_v, state_delta, state_conv, page_table, c_prompt, c_sample,
           w_ada, b_ada, norm_mix, w_in, conv_w, a_log, dt_bias, delta_norm, w_branch_a, w_branch_b, w_out,
           rel_bias, norm_ffn, router_w, router_bias, w_gate_e, w_up_e, w_down_e, w_gate_s, w_up_s, w_down_s,
           norm_final):
    depth = w_ada.shape[0]
    assert depth == 1, "the final norm is fused into the layer's last stage"
    n_b, seq, d = x_prompt.shape
    d_b, d_seq, _ = x_sample.shape
    n_pages = page_table.shape[1]
    page = cache_k.shape[2]
    past_len = n_pages * page
    a_width = A_HEADS * HEAD_DIM
    g_width = G_HEADS * HEAD_DIM
    conv_ch = 3 * g_width
    l = 0

    off_small = 3 * a_width + conv_ch + g_width
    w_l = w_in[l]
    w_main = jnp.concatenate([w_l[:, :off_small], w_l[:, off_small + 2 * G_HEADS:]], axis=1).astype(BF16)
    w_small = jnp.pad(w_l[:, off_small:off_small + 2 * G_HEADS], ((0, 0), (0, LANES - 2 * G_HEADS))).astype(BF16)
    assert w_main.shape[1] == IN_COL_TILES * IN_TILE

    c_all = jnp.concatenate([c_prompt, c_sample], axis=0)
    mod = ada_mod(c_all, w_ada[l], b_ada[l])
    mod_p = mod[:n_b].reshape(n_b, N_ADA, 1, d)
    mod_s = mod[n_b:].reshape(d_b, N_ADA, 1, d)

    tiles_p = (1, 512)
    tiles_s = (512 // d_seq, d_seq)
    post_p = (1, 256)
    post_s = (256 // d_seq, d_seq)
    wa = w_branch_a[l].astype(BF16)
    wb = w_branch_b[l].astype(BF16)
    wo = w_out[l].astype(BF16)
    rw = jnp.pad(router_w[l], ((0, 0), (0, LANES - router_w.shape[-1])))

    def mixer(x, mod4, tiles, post_tiles, attend, s0, buf0, chunk):
        a_q, a_k, a_v, g_qkv, g_z, gate_a, gate_b, small = in_proj(x, mod4, norm_mix[l], w_main, w_small, *tiles)
        y_a = attend(a_q, a_k, a_v)
        small_t = jnp.swapaxes(small[..., :2 * G_HEADS], 1, 2)
        y_b, s_new, tail = delta_rule(g_qkv, g_z, small, small_t, conv_w[l], a_log[l], dt_bias[l],
                                      delta_norm[l], s0, buf0, chunk)
        x1, h2, lg = post_mix(y_a, y_b, gate_a, gate_b, x, mod4, wa, wb, wo, norm_ffn[l], rw, *post_tiles)
        return x1, h2, lg, a_k, a_v, s_new, tail[:, SUBLANES - (CONV_W - 1):]

    s0_p = jnp.zeros((n_b, G_HEADS, HEAD_DIM, HEAD_DIM), F32)
    buf0_p = jnp.zeros((n_b, SUBLANES, conv_ch), F32)
    buf0_s = jnp.pad(state_conv[l], ((0, 0), (SUBLANES - (CONV_W - 1), 0), (0, 0)))

    x1_p, h2_p, lg_p, k_p, v_p, d_p, cv_p = mixer(
        x_prompt, mod_p, tiles_p, post_p, functools.partial(moba_prompt, rel_bias=rel_bias), s0_p, buf0_p,
        DELTA_CHUNK)
    x1_s, h2_s, lg_s, k_s, v_s, d_s, cv_s = mixer(
        x_sample, mod_s, tiles_s, post_s,
        functools.partial(moba_sample, cache_k=cache_k, cache_v=cache_v, layer=l, page_table=page_table,
                          rel_bias=rel_bias, past_len=past_len),
        state_delta[l], buf0_s, d_seq)

    comb_p = (1, 256)
    comb_s = (256 // d_seq, d_seq)
    y_p, y_s = _moe(h2_p, h2_s, x1_p, x1_s, lg_p, lg_s, mod_p, mod_s, router_bias[l],
                    w_gate_e[l], w_up_e[l], w_down_e[l],
                    w_gate_s[l].astype(BF16), w_up_s[l].astype(BF16), w_down_s[l].astype(BF16),
                    norm_final, comb_p, comb_s)

    def heads(t, n_heads):
        return t.reshape(t.shape[:-1] + (n_heads, HEAD_DIM))[None]

    return (y_p, y_s, heads(k_p, A_HEADS), heads(v_p, A_HEADS), heads(k_s, A_HEADS), heads(v_s, A_HEADS),
            d_p[None], cv_p[None], d_s[None], cv_s[None])
```

```python
import functools
import math

import numpy as np
import jax
import jax.numpy as jnp
from jax import lax
from jax.experimental import pallas as pl
from jax.experimental.pallas import tpu as pltpu

F32 = jnp.float32
BF16 = jnp.bfloat16
I32 = jnp.int32
EPS = 1e-6
NEG_INF = float("-inf")

A_HEADS = 8
G_HEADS = 8
MOBA_BLOCK = 256
MOBA_TOPK = 3
REL_BUCKETS = 32
REL_MAX_DIST = 128
CONV_W = 4
N_GROUPS = 8
TOPK_GROUPS = 4
TOP_K = 6
ROUTED_SCALE = 2.5
N_ADA = 6
HEAD_DIM = 128

LANES = 128
SUBLANES = 8
VMEM_LIMIT = 56 * 1024 * 1024
EXPERT_ROWS = 256
DELTA_CHUNK = 128
SAMPLE_BLOCKS = 8


def _cparams(sem, vmem=VMEM_LIMIT):
    return pltpu.CompilerParams(dimension_semantics=sem, vmem_limit_bytes=vmem)


def _dot(a, b):
    return jnp.dot(a.astype(BF16), b.astype(BF16), preferred_element_type=F32)


def _dot_nt(a, b):
    return lax.dot_general(a.astype(BF16), b.astype(BF16), (((1,), (1,)), ((), ())),
                           preferred_element_type=F32)


def _dot_tn(a, b):
    return lax.dot_general(a.astype(BF16), b.astype(BF16), (((0,), (0,)), ((), ())),
                           preferred_element_type=F32)


def _dot_exact(a, b):
    return jnp.dot(a, b, precision=lax.Precision.HIGHEST, preferred_element_type=F32)


_DIMS = {"nn": (((1,), (0,)), ((), ())), "nt": (((1,), (1,)), ((), ())), "tn": (((0,), (0,)), ((), ()))}

def _mm(a, b, form, passes):
    dims = _DIMS[form]
    a_hi = a.astype(BF16)
    b_hi = b.astype(BF16)
    out = lax.dot_general(a_hi, b_hi, dims, preferred_element_type=F32)
    if passes == 3:
        a_lo = (a - a_hi.astype(F32)).astype(BF16)
        b_lo = (b - b_hi.astype(F32)).astype(BF16)
        out = out + (lax.dot_general(a_hi, b_lo, dims, preferred_element_type=F32)
                     + lax.dot_general(a_lo, b_hi, dims, preferred_element_type=F32))
    return out


def _pack_pairs(x):
    w = x.shape[-1] // 2
    lo = lax.bitcast_convert_type(x[:, :w].astype(BF16).astype(F32), jnp.uint32)
    hi = lax.bitcast_convert_type(x[:, w:].astype(BF16).astype(F32), jnp.uint32)
    return (lo >> 16) | (hi & jnp.uint32(0xFFFF0000))


def _unpack_pairs(u):
    lo = lax.bitcast_convert_type(u << 16, F32)
    hi = lax.bitcast_convert_type(u & jnp.uint32(0xFFFF0000), F32)
    return lo, hi


def _silu(x):
    return x * jax.nn.sigmoid(x)


def _rel_bucket_np(dist):
    n = np.maximum(dist, 0)
    max_exact = REL_BUCKETS // 2
    nf = np.maximum(n, 1).astype(np.float32)
    large = max_exact + (np.log(nf / np.float32(max_exact)) / np.float32(math.log(REL_MAX_DIST / max_exact))
                         * np.float32(REL_BUCKETS - max_exact)).astype(np.int32)
    return np.where(n < max_exact, n, np.minimum(large, REL_BUCKETS - 1)).astype(np.int32)


def _ada_kernel(c_ref, w_ref, b_ref, o_ref):
    o_ref[...] = _dot(_silu(c_ref[...]), w_ref[...]) + b_ref[...]


def ada_mod(c_all, w_ada, b_ada):
    rows, d = c_all.shape
    n = w_ada.shape[1]
    tn = 1024
    return pl.pallas_call(
        _ada_kernel,
        out_shape=jax.ShapeDtypeStruct((rows, n), F32),
        grid=(n // tn,),
        in_specs=[pl.BlockSpec((rows, d), lambda j: (0, 0)),
                  pl.BlockSpec((d, tn), lambda j: (0, j)),
                  pl.BlockSpec((1, tn), lambda j: (0, j))],
        out_specs=pl.BlockSpec((rows, tn), lambda j: (0, j)),
        compiler_params=_cparams(("arbitrary",)),
        name="ada_mod",
    )(c_all, w_ada, b_ada.reshape(1, n))


IN_TILE = 1024
IN_GROUPS = (("a_q", 0, 1), ("a_k", 1, 1), ("a_v", 2, 1), ("g_qkv", 3, 3), ("g_z", 6, 1),
             ("gate_a", 7, 2), ("gate_b", 9, 2))
IN_COL_TILES = 11


def _inproj_kernel(x_ref, sc_ref, sh_ref, nw_ref, w_ref, ws_ref, *rest):
    out_refs = rest[:len(IN_GROUPS)]
    small_ref = rest[len(IN_GROUPS)]
    h_ref = rest[len(IN_GROUPS) + 1]
    bt, tl, d = x_ref.shape
    j = pl.program_id(2)

    @pl.when(j == 0)
    def _():
        x = x_ref[...]
        y = x * lax.rsqrt(jnp.mean(x * x, axis=-1, keepdims=True) + EPS) * nw_ref[...]
        h = y * (1.0 + sc_ref[:, 0]) + sh_ref[:, 0]
        h2 = h.reshape(bt * tl, d).astype(BF16)
        h_ref[...] = h2
        small_ref[...] = jnp.dot(h2, ws_ref[...], preferred_element_type=F32).reshape(bt, tl, LANES)

    res = jnp.dot(h_ref[...], w_ref[...], preferred_element_type=F32).reshape(bt, tl, IN_TILE)
    for o_ref, (_, j0, nj) in zip(out_refs, IN_GROUPS):
        @pl.when((j >= j0) & (j < j0 + nj))
        def _(o_ref=o_ref):
            o_ref[...] = res


def in_proj(x, mod4, norm_w, w_main, w_small, bt, tl):
    b, l, d = x.shape
    grid = (b // bt, l // tl, IN_COL_TILES)
    out_shapes, out_specs = [], []
    for _, j0, nj in IN_GROUPS:
        out_shapes.append(jax.ShapeDtypeStruct((b, l, nj * IN_TILE), F32))
        out_specs.append(pl.BlockSpec((bt, tl, IN_TILE),
                                      lambda i, t, j, j0=j0, nj=nj: (i, t, jnp.clip(j - j0, 0, nj - 1))))
    out_shapes.append(jax.ShapeDtypeStruct((b, l, LANES), F32))
    out_specs.append(pl.BlockSpec((bt, tl, LANES), lambda i, t, j: (i, t, 0)))
    return pl.pallas_call(
        _inproj_kernel,
        out_shape=out_shapes,
        grid=grid,
        in_specs=[pl.BlockSpec((bt, tl, d), lambda i, t, j: (i, t, 0)),
                  pl.BlockSpec((bt, 1, 1, d), lambda i, t, j: (i, 1, 0, 0)),
                  pl.BlockSpec((bt, 1, 1, d), lambda i, t, j: (i, 0, 0, 0)),
                  pl.BlockSpec((1, d), lambda i, t, j: (0, 0)),
                  pl.BlockSpec((d, IN_TILE), lambda i, t, j: (0, j)),
                  pl.BlockSpec((d, LANES), lambda i, t, j: (0, 0))],
        out_specs=out_specs,
        scratch_shapes=[pltpu.VMEM((bt * tl, d), BF16)],
        compiler_params=_cparams(("arbitrary", "arbitrary", "arbitrary")),
        name="in_proj",
    )(x, mod4, mod4, norm_w.reshape(1, d), w_main, w_small)


def _softmax_step(s, v_blk, m, l, acc):
    m_new = jnp.maximum(m, jnp.max(s, axis=-1, keepdims=True))
    alpha = jnp.exp(m - m_new)
    p = jnp.exp(s - m_new)
    l = alpha * l + jnp.sum(p, axis=-1, keepdims=True)
    acc = alpha * acc + _dot(p, v_blk)
    return m_new, l, acc


def _topk_select(cols, k):
    sels = []
    for n, gn in enumerate(cols):
        rank = jnp.zeros(gn.shape, F32)
        for m_, gm in enumerate(cols):
            if m_ == n:
                continue
            ahead = (gm >= gn) if m_ < n else (gm > gn)
            rank = rank + jnp.where(ahead, 1.0, 0.0)
        sels.append(rank < float(k))
    return sels


def _moba_prompt_kernel(rb_ref, bkt_ref, q_ref, k_ref, v_ref, o_ref, bias_ref):
    h = pl.program_id(0)
    s_len = q_ref.shape[1]
    blk = MOBA_BLOCK
    nb = s_len // blk
    scale = HEAD_DIM ** -0.5

    @pl.when(pl.program_id(1) == 0)
    def _():
        row = lax.broadcasted_iota(I32, (blk, blk), 0)
        col = lax.broadcasted_iota(I32, (blk, blk), 1)
        for t in range(2):
            bkt = bkt_ref[t]
            bias = jnp.zeros((blk, blk), F32)
            for r in range(REL_BUCKETS):
                bias = jnp.where(bkt == r, rb_ref[r, h], bias)
            bias_ref[t] = jnp.where(col <= row, bias, NEG_INF) if t == 0 else bias

    bias_far = rb_ref[REL_BUCKETS - 1, h]
    k_means = [jnp.mean(k_ref[0, n * blk:(n + 1) * blk, :], axis=0, keepdims=True) for n in range(nb)]

    for qb in range(nb):
        q = q_ref[0, qb * blk:(qb + 1) * blk, :]
        qs = (q * scale).astype(BF16)
        if qb > MOBA_TOPK:
            gates = [jnp.sum(q * k_means[n], axis=-1, keepdims=True) for n in range(qb)]
            masks = [jnp.where(sel, 0.0, NEG_INF) for sel in _topk_select(gates, MOBA_TOPK)]
        else:
            masks = [None] * qb
        s = _dot_nt(qs, k_ref[0, qb * blk:(qb + 1) * blk, :]) + bias_ref[0]
        m = jnp.max(s, axis=-1, keepdims=True)
        p = jnp.exp(s - m)
        l = jnp.sum(p, axis=-1, keepdims=True)
        acc = _dot(p, v_ref[0, qb * blk:(qb + 1) * blk, :])
        for n in range(qb - 1, -1, -1):
            s = _dot_nt(qs, k_ref[0, n * blk:(n + 1) * blk, :])
            if n == qb - 1:
                s = s + bias_ref[1]
                if masks[n] is not None:
                    s = s + masks[n]
            else:
                s = s + (bias_far if masks[n] is None else masks[n] + bias_far)
            m, l, acc = _softmax_step(s, v_ref[0, n * blk:(n + 1) * blk, :], m, l, acc)
        o_ref[0, qb * blk:(qb + 1) * blk, :] = acc / l


def moba_prompt(q, k, v, rel_bias):
    b, s_len, width = q.shape
    assert s_len % MOBA_BLOCK == 0 and width == A_HEADS * HEAD_DIM
    ar = np.arange(MOBA_BLOCK)
    d_loc = ar[:, None] - ar[None, :]
    bkt = np.stack([_rel_bucket_np(d_loc), _rel_bucket_np(d_loc + MOBA_BLOCK)]).astype(np.int32)
    assert int(_rel_bucket_np(np.array([MOBA_BLOCK + 1]))[0]) == REL_BUCKETS - 1
    spec = pl.BlockSpec((1, s_len, HEAD_DIM), lambda h, i: (i, 0, h))
    return pl.pallas_call(
        _moba_prompt_kernel,
        out_shape=jax.ShapeDtypeStruct((b, s_len, width), F32),
        grid=(A_HEADS, b),
        in_specs=[pl.BlockSpec(memory_space=pltpu.SMEM),
                  pl.BlockSpec((2, MOBA_BLOCK, MOBA_BLOCK), lambda h, i: (0, 0, 0)),
                  spec, spec, spec],
        out_specs=spec,
        scratch_shapes=[pltpu.VMEM((2, MOBA_BLOCK, MOBA_BLOCK), F32)],
        compiler_params=_cparams(("arbitrary", "arbitrary")),
        name="moba_prompt",
    )(rel_bias, jnp.asarray(bkt), q, k, v)


def _moba_sample_kernel(pt_ref, rb_ref, bkt_ref, q_ref, kn_ref, vn_ref, *rest, past_len):
    nh = A_HEADS
    n_pg = 2 * SAMPLE_BLOCKS
    k_pages_all = rest[0:n_pg]
    v_pages_all = rest[n_pg:2 * n_pg]
    o_ref, bias_ref, m_ref, l_ref, g_ref, acc_ref = rest[2 * n_pg:]
    page = k_pages_all[0].shape[1] // nh

    def head_rows(ref, h):
        return ref[0, pl.ds(h, page, stride=nh), :]

    n = pl.program_id(1)
    nb = pl.num_programs(1)
    t = q_ref.shape[1]
    dh = HEAD_DIM
    blk = MOBA_BLOCK
    scale = dh ** -0.5
    n_last = past_len // blk - 1

    @pl.when(n == 0)
    def _():
        for h in range(nh):
            bkt = bkt_ref[...]
            bias = jnp.zeros((t, blk), F32)
            for r in range(REL_BUCKETS):
                bias = jnp.where(bkt == r, rb_ref[r, h], bias)
            bias_ref[h * t:(h + 1) * t, :] = bias

    q_all = q_ref[0]
    qhs = [q_all[:, h * dh:(h + 1) * dh] for h in range(nh)]
    qss = [(qh * scale).astype(BF16) for qh in qhs]
    heads = range(nh)
    for jb in range(SAMPLE_BLOCKS):
        gi = n * SAMPLE_BLOCKS + jb
        k_pages = k_pages_all[2 * jb:2 * jb + 2]
        v_pages = v_pages_all[2 * jb:2 * jb + 2]
        k_blks = [jnp.concatenate([head_rows(k_pages[0], h), head_rows(k_pages[1], h)], axis=0) for h in heads]
        v_blks = [jnp.concatenate([head_rows(v_pages[0], h), head_rows(v_pages[1], h)], axis=0) for h in heads]
        ss = [_dot_nt(qss[h], k_blks[h]) for h in heads]
        gs = [jnp.sum(qhs[h] * jnp.mean(k_blks[h], axis=0, keepdims=True), axis=-1, keepdims=True)
              for h in heads]
        ss = [ss[h] + jnp.where(gi == n_last, bias_ref[h * t:(h + 1) * t, :], rb_ref[REL_BUCKETS - 1, h])
              for h in heads]
        ms = [jnp.max(s, axis=-1, keepdims=True) for s in ss]
        ps = [jnp.exp(s - m) for s, m in zip(ss, ms)]
        accs = [_dot(ps[h], v_blks[h]) for h in heads]
        m_ref[gi] = jnp.concatenate(ms, axis=0)
        l_ref[gi] = jnp.concatenate([jnp.sum(p, axis=-1, keepdims=True) for p in ps], axis=0)
        g_ref[gi] = jnp.concatenate(gs, axis=0)
        acc_ref[gi] = jnp.concatenate(accs, axis=0)

    @pl.when(n == nb - 1)
    def _():
        n_blocks = past_len // blk
        rows = nh * t
        sels = _topk_select([g_ref[i] for i in range(n_blocks)], MOBA_TOPK)
        tq = lax.broadcasted_iota(I32, (t, t), 0)
        tk = lax.broadcasted_iota(I32, (t, t), 1)
        s_locs = []
        for h in range(nh):
            bias_loc = jnp.zeros((t, t), F32)
            for d in range(t):
                bias_loc = jnp.where(tq - tk == d, rb_ref[d, h], bias_loc)
            s_loc = _dot_nt(qss[h], kn_ref[0, :, h * dh:(h + 1) * dh]) + bias_loc
            s_locs.append(jnp.where(tk <= tq, s_loc, NEG_INF))
        s_loc = jnp.concatenate(s_locs, axis=0)
        m_tot = jnp.max(s_loc, axis=-1, keepdims=True)
        for i in range(n_blocks):
            m_tot = jnp.maximum(m_tot, jnp.where(sels[i], m_ref[i], NEG_INF))
        p_loc = jnp.exp(s_loc - m_tot)
        l_tot = jnp.sum(p_loc, axis=-1, keepdims=True)
        acc = jnp.concatenate([_dot(p_loc[h * t:(h + 1) * t, :], vn_ref[0, :, h * dh:(h + 1) * dh])
                               for h in range(nh)], axis=0)
        for i in range(n_blocks):
            w = jnp.where(sels[i], jnp.exp(jnp.where(sels[i], m_ref[i] - m_tot, 0.0)), 0.0)
            l_tot = l_tot + w * l_ref[i]
            acc = acc + w * acc_ref[i]
        out = acc / l_tot
        o_ref[0] = jnp.concatenate([out[h * t:(h + 1) * t, :] for h in range(nh)], axis=1)


def moba_sample(q, k_new, v_new, cache_k, cache_v, layer, page_table, rel_bias, past_len):
    db, t, width = q.shape
    page = cache_k.shape[2]
    assert MOBA_BLOCK == 2 * page and past_len % MOBA_BLOCK == 0 and t <= REL_BUCKETS // 2
    assert past_len // MOBA_BLOCK >= MOBA_TOPK and t % SUBLANES == 0
    n_blocks = past_len // MOBA_BLOCK
    assert n_blocks % SAMPLE_BLOCKS == 0
    rows = A_HEADS * t
    d_last = MOBA_BLOCK + np.arange(t)[:, None] - np.arange(MOBA_BLOCK)[None, :]
    bkt = _rel_bucket_np(d_last).astype(np.int32)
    assert int(_rel_bucket_np(np.array([MOBA_BLOCK + 1]))[0]) == REL_BUCKETS - 1
    tok_spec = pl.BlockSpec((1, t, width), lambda i, n, pt: (i, 0, 0))

    n_phys = cache_k.shape[1]
    ck = cache_k.reshape(cache_k.shape[0] * n_phys, page * A_HEADS, HEAD_DIM)
    cv = cache_v.reshape(cache_v.shape[0] * n_phys, page * A_HEADS, HEAD_DIM)

    n_pg = 2 * SAMPLE_BLOCKS
    page_specs = [pl.BlockSpec((1, page * A_HEADS, HEAD_DIM),
                               lambda i, n, pt, j=j: (layer * n_phys + pt[i, n_pg * n + j], 0, 0))
                  for j in range(n_pg)]

    grid_spec = pltpu.PrefetchScalarGridSpec(
        num_scalar_prefetch=1,
        grid=(db, n_blocks // SAMPLE_BLOCKS),
        in_specs=[pl.BlockSpec(memory_space=pltpu.SMEM),
                  pl.BlockSpec((t, MOBA_BLOCK), lambda i, n, pt: (0, 0)),
                  tok_spec, tok_spec, tok_spec] + page_specs + page_specs,
        out_specs=tok_spec,
        scratch_shapes=[pltpu.VMEM((rows, MOBA_BLOCK), F32),
                        pltpu.VMEM((n_blocks, rows, 1), F32),
                        pltpu.VMEM((n_blocks, rows, 1), F32),
                        pltpu.VMEM((n_blocks, rows, 1), F32),
                        pltpu.VMEM((n_blocks, rows, HEAD_DIM), F32)],
    )
    return pl.pallas_call(
        functools.partial(_moba_sample_kernel, past_len=past_len),
        out_shape=jax.ShapeDtypeStruct((db, t, width), F32),
        grid_spec=grid_spec,
        compiler_params=_cparams(("arbitrary", "arbitrary")),
        name="moba_sample",
    )(page_table, rel_bias, jnp.asarray(bkt), q, k_new, v_new, *([ck] * n_pg + [cv] * n_pg))


INV_BASE = 16


def _unit_lower_inverses(a_mats, ri, ci, size):
    base = min(INV_BASE, size)
    same = ri // base == ci // base
    eye = jnp.where(ri == ci, 1.0, 0.0)
    pws = [jnp.where(same, a, 0.0) for a in a_mats]
    invs = [eye - p for p in pws]
    for _ in range(max(int(math.log2(base)) - 1, 0)):
        pws = [_mm(p, p, "nn", 1) for p in pws]
        invs = [i + _mm(i, p, "nn", 1) for i, p in zip(invs, pws)]
    blk = base
    while blk < size:
        pr = ri // blk
        pc = ci // blk
        join = (pr == pc + 1) & (pr // 2 == pc // 2)
        halves = [_mm(i, jnp.where(join, a, 0.0), "nn", 1) for i, a in zip(invs, a_mats)]
        invs = [i - _mm(hf, i, "nn", 1) for i, hf in zip(invs, halves)]
        blk *= 2
    return invs


def _delta_kernel(x_ref, cw_ref, z_ref, sm_ref, smt_ref, ar_ref, dr_ref, ac_ref, dc_ref, nw_ref, s0_ref, b0_ref,
                  y_ref, sn_ref, tail_ref, s_sc, carry_sc):
    c = pl.program_id(1)
    nc = pl.num_programs(1)
    chunk = x_ref.shape[1]
    dk = HEAD_DIM
    hh = G_HEADS
    part = hh * dk

    @pl.when(c == 0)
    def _():
        s_sc[...] = s0_ref[0]
        carry_sc[...] = b0_ref[0]

    x = x_ref[0]
    xx = jnp.concatenate([carry_sc[...], x], axis=0)
    cw = cw_ref[...]
    y = x * cw[CONV_W - 1:CONV_W, :]
    for s in range(1, CONV_W):
        y = y + xx[SUBLANES - s:SUBLANES - s + chunk, :] * cw[CONV_W - 1 - s:CONV_W - s, :]
    carry_sc[...] = xx[chunk:chunk + SUBLANES, :]
    qkv = _silu(y)

    def softplus(t):
        return jnp.maximum(t, 0.0) + jnp.log(1.0 + jnp.exp(-jnp.abs(t)))

    sm = sm_ref[0]
    beta_all = jax.nn.sigmoid(sm)
    g_cols = -jnp.exp(ar_ref[...]) * softplus(sm + dr_ref[...])
    g_rows = -jnp.exp(ac_ref[...]) * softplus(smt_ref[0] + dc_ref[...])
    ri = lax.broadcasted_iota(I32, (chunk, chunk), 0)
    ci = lax.broadcasted_iota(I32, (chunk, chunk), 1)
    incl = ci <= ri
    strict = ci < ri
    gc_cols = _dot_exact(jnp.where(incl, 1.0, 0.0), g_cols)
    gc_rows = _dot_exact(g_rows, jnp.where(ri <= ci, 1.0, 0.0))
    nw = nw_ref[...]

    heads = range(hh)
    qs, ks, vs, betas, gcs, dmasks = [], [], [], [], [], []
    for h in heads:
        q = qkv[:, h * dk:(h + 1) * dk]
        k = qkv[:, part + h * dk:part + (h + 1) * dk]
        qs.append(q * lax.rsqrt(jnp.sum(q * q, axis=-1, keepdims=True) + EPS) * (dk ** -0.5))
        ks.append(k * lax.rsqrt(jnp.sum(k * k, axis=-1, keepdims=True) + EPS))
        vs.append(qkv[:, 2 * part + h * dk:2 * part + (h + 1) * dk])
        betas.append(beta_all[:, h:h + 1])
        gc_c = gc_cols[:, hh + h:hh + h + 1]
        gc_r = gc_rows[hh + h:hh + h + 1, :]
        gcs.append(gc_c)
        dmasks.append(jnp.exp(jnp.where(incl, gc_c - gc_r, NEG_INF)))

    kbs = [k * b for k, b in zip(ks, betas)]
    a_mats = [jnp.where(strict, _mm(kb, k, "nt", 1) * dm, 0.0) for kb, k, dm in zip(kbs, ks, dmasks)]
    t_invs = _unit_lower_inverses(a_mats, ri, ci, chunk)
    egs = [jnp.exp(g) for g in gcs]
    uws = [_mm(t, jnp.concatenate([v * b, kb * eg], axis=1), "nn", 1)
           for t, v, b, kb, eg in zip(t_invs, vs, betas, kbs, egs)]
    s_mats = [s_sc[h] for h in heads]
    v_news = [uw[:, :dk] - _mm(uw[:, dk:], s, "nn", 1) for uw, s in zip(uws, s_mats)]
    attns = [_mm(q, k, "nt", 1) * dm for q, k, dm in zip(qs, ks, dmasks)]
    outs = [_mm(q * eg, s, "nn", 1) + _mm(at, vn, "nn", 1)
            for q, eg, s, at, vn in zip(qs, egs, s_mats, attns, v_news)]
    for h in heads:
        g_last = gcs[h][chunk - 1:chunk, :]
        s_sc[h] = s_mats[h] * jnp.exp(g_last) + _mm(ks[h] * jnp.exp(g_last - gcs[h]), v_news[h], "tn", 1)
    for h in heads:
        o = outs[h]
        o_n = o * lax.rsqrt(jnp.mean(o * o, axis=-1, keepdims=True) + EPS) * nw
        y_ref[0, :, h * dk:(h + 1) * dk] = o_n * _silu(z_ref[0, :, h * dk:(h + 1) * dk])

    @pl.when(c == nc - 1)
    def _():
        sn_ref[0] = s_sc[...]
        tail_ref[0] = carry_sc[...]


def delta_rule(xg, z, small, small_t, conv_w, a_log, dt_bias, delta_norm, s0, buf0, chunk):
    b, l, cw_ch = xg.shape
    hh = G_HEADS
    assert l % chunk == 0 and cw_ch == 3 * hh * HEAD_DIM and l >= CONV_W - 1
    nc = l // chunk
    width = hh * HEAD_DIM
    a_row = jnp.pad(a_log.reshape(1, hh), ((0, 0), (hh, LANES - 2 * hh)))
    d_row = jnp.pad(dt_bias.reshape(1, hh), ((0, 0), (hh, LANES - 2 * hh)))
    a_col = jnp.pad(a_log.reshape(hh, 1), ((hh, 0), (0, 0)))
    d_col = jnp.pad(dt_bias.reshape(hh, 1), ((hh, 0), (0, 0)))

    def const(shape):
        return pl.BlockSpec(shape, lambda i, c: (0,) * len(shape))

    return pl.pallas_call(
        _delta_kernel,
        out_shape=[jax.ShapeDtypeStruct((b, l, width), F32),
                   jax.ShapeDtypeStruct((b, hh, HEAD_DIM, HEAD_DIM), F32),
                   jax.ShapeDtypeStruct((b, SUBLANES, cw_ch), F32)],
        grid=(b, nc),
        in_specs=[pl.BlockSpec((1, chunk, cw_ch), lambda i, c: (i, c, 0)),
                  const((CONV_W, cw_ch)),
                  pl.BlockSpec((1, chunk, width), lambda i, c: (i, c, 0)),
                  pl.BlockSpec((1, chunk, LANES), lambda i, c: (i, c, 0)),
                  pl.BlockSpec((1, 2 * hh, chunk), lambda i, c: (i, 0, c)),
                  const((1, LANES)), const((1, LANES)), const((2 * hh, 1)), const((2 * hh, 1)),
                  const((1, HEAD_DIM)),
                  pl.BlockSpec((1, hh, HEAD_DIM, HEAD_DIM), lambda i, c: (i, 0, 0, 0)),
                  pl.BlockSpec((1, SUBLANES, cw_ch), lambda i, c: (i, 0, 0))],
        out_specs=[pl.BlockSpec((1, chunk, width), lambda i, c: (i, c, 0)),
                   pl.BlockSpec((1, hh, HEAD_DIM, HEAD_DIM), lambda i, c: (i, 0, 0, 0)),
                   pl.BlockSpec((1, SUBLANES, cw_ch), lambda i, c: (i, 0, 0))],
        scratch_shapes=[pltpu.VMEM((hh, HEAD_DIM, HEAD_DIM), F32),
                        pltpu.VMEM((SUBLANES, cw_ch), F32)],
        compiler_params=_cparams(("arbitrary", "arbitrary")),
        name="delta_rule",
    )(xg, conv_w, z, small, small_t, a_row, d_row, a_col, d_col, delta_norm.reshape(1, HEAD_DIM), s0, buf0)


def _post_kernel(ya_ref, yb_ref, ga_ref, gb_ref, x_ref, gm_ref, scf_ref, shf_ref, wa_ref, wb_ref, wo_ref,
                 nw_ref, rw_ref, x1_ref, h2_ref, lg_ref):
    bt, tl, d = x_ref.shape
    rows = bt * tl
    ya = ya_ref[...].reshape(rows, -1)
    yb = yb_ref[...].reshape(rows, -1)
    merged = (jax.nn.sigmoid(ga_ref[...].reshape(rows, d)) * _dot(ya, wa_ref[...])
              + jax.nn.sigmoid(gb_ref[...].reshape(rows, d)) * _dot(yb, wb_ref[...]))
    mix = _dot(merged, wo_ref[...]).reshape(bt, tl, d)
    x1 = x_ref[...] + gm_ref[:, 0] * mix
    x1_ref[...] = x1
    y = x1 * lax.rsqrt(jnp.mean(x1 * x1, axis=-1, keepdims=True) + EPS) * nw_ref[...]
    h2 = (y * (1.0 + scf_ref[:, 0]) + shf_ref[:, 0]).reshape(rows, d)
    h2_ref[...] = _pack_pairs(h2).reshape(bt, tl, d // 2)
    lg_ref[...] = _mm(h2, rw_ref[...], "nn", 3).reshape(bt, tl, -1)


def post_mix(y_a, y_b, gate_a, gate_b, x, mod4, w_a, w_b, w_o, norm_w, router_w, bt, tl):
    b, l, d = x.shape
    wa_in = y_a.shape[-1]
    wb_in = y_b.shape[-1]
    n_e = router_w.shape[1]

    def tok(width):
        return pl.BlockSpec((bt, tl, width), lambda i, t: (i, t, 0))

    def modspec(idx):
        return pl.BlockSpec((bt, 1, 1, d), lambda i, t, idx=idx: (i, idx, 0, 0))

    def const(shape):
        return pl.BlockSpec(shape, lambda i, t: (0, 0), pipeline_mode=pl.Buffered(1))

    return pl.pallas_call(
        _post_kernel,
        out_shape=[jax.ShapeDtypeStruct((b, l, d), F32), jax.ShapeDtypeStruct((b, l, d // 2), jnp.uint32),
                   jax.ShapeDtypeStruct((b, l, n_e), F32)],
        grid=(b // bt, l // tl),
        in_specs=[tok(wa_in), tok(wb_in), tok(d), tok(d), tok(d), modspec(2), modspec(4), modspec(3),
                  const((wa_in, d)), const((wb_in, d)), const((d, d)), const((1, d)), const((d, n_e))],
        out_specs=[tok(d), tok(d // 2), tok(n_e)],
        compiler_params=_cparams(("arbitrary", "arbitrary")),
        name="post_mix",
    )(y_a, y_b, gate_a, gate_b, x, mod4, mod4, mod4, w_a, w_b, w_o, norm_w.reshape(1, d), router_w)


def _route_kernel(lg_ref, rb_ref, e_ref, w_ref, p_ref, cnt_ref, run_sc, *, n_e):
    i = pl.program_id(0)
    tm = lg_ref.shape[0]
    per_group = n_e // N_GROUPS
    assert per_group == SUBLANES

    @pl.when(i == 0)
    def _():
        run_sc[...] = jnp.zeros_like(run_sc)

    scores = jax.nn.sigmoid(lg_ref[...].T[:n_e, :])
    biased = scores + rb_ref[...]
    sub = lax.broadcasted_iota(I32, (per_group, tm), 0)

    group_scores = []
    for g in range(N_GROUPS):
        xg = biased[g * per_group:(g + 1) * per_group, :]
        m1 = jnp.max(xg, axis=0, keepdims=True)
        i1 = jnp.min(jnp.where(xg == m1, sub, per_group), axis=0, keepdims=True)
        m2 = jnp.max(jnp.where(sub == i1, NEG_INF, xg), axis=0, keepdims=True)
        group_scores.append(m1 + m2)
    keep = _topk_select(group_scores, TOPK_GROUPS)
    masked = jnp.concatenate(
        [jnp.where(keep[g], biased[g * per_group:(g + 1) * per_group, :], NEG_INF) for g in range(N_GROUPS)], axis=0)

    eidx = lax.broadcasted_iota(I32, (n_e, tm), 0)
    sel = jnp.zeros((n_e, tm), jnp.bool_)
    hits = []
    idxs = []
    for _ in range(TOP_K):
        mx = jnp.max(masked, axis=0, keepdims=True)
        idx = jnp.min(jnp.where(masked == mx, eidx, n_e), axis=0, keepdims=True)
        hit = eidx == idx
        sel = sel | hit
        masked = jnp.where(hit, NEG_INF, masked)
        hits.append(hit)
        idxs.append(idx)
    sel_f = jnp.where(sel, 1.0, 0.0)
    top_w = scores * sel_f
    top_w = top_w / jnp.sum(top_w, axis=0, keepdims=True) * ROUTED_SCALE

    ri = lax.broadcasted_iota(I32, (tm, tm), 0)
    ci = lax.broadcasted_iota(I32, (tm, tm), 1)
    before = _dot(sel_f, jnp.where(ri < ci, 1.0, 0.0)) + run_sc[...]
    run_sc[...] = run_sc[...] + jnp.sum(sel_f, axis=1, keepdims=True)
    cnt_ref[...] = run_sc[...].astype(I32)

    pad_rows = SUBLANES - TOP_K
    e_ref[...] = jnp.concatenate(idxs + [jnp.zeros((pad_rows, tm), I32)], axis=0)
    w_ref[...] = jnp.concatenate([jnp.sum(jnp.where(h, top_w, 0.0), axis=0, keepdims=True) for h in hits]
                                 + [jnp.zeros((pad_rows, tm), F32)], axis=0)
    p_ref[...] = jnp.concatenate([jnp.sum(jnp.where(h, before, 0.0), axis=0, keepdims=True) for h in hits]
                                 + [jnp.zeros((pad_rows, tm), F32)], axis=0).astype(I32)


def route(logits, router_bias, n_e, tm):
    n, width = logits.shape
    assert n % tm == 0 and width == LANES and n_e <= LANES and TOP_K <= SUBLANES
    tok_t = pl.BlockSpec((SUBLANES, tm), lambda i: (0, i))
    col = pl.BlockSpec((n_e, 1), lambda i: (0, 0))
    return pl.pallas_call(
        functools.partial(_route_kernel, n_e=n_e),
        out_shape=[jax.ShapeDtypeStruct((SUBLANES, n), I32), jax.ShapeDtypeStruct((SUBLANES, n), F32),
                   jax.ShapeDtypeStruct((SUBLANES, n), I32), jax.ShapeDtypeStruct((n_e, 1), I32)],
        grid=(n // tm,),
        in_specs=[pl.BlockSpec((tm, LANES), lambda i: (i, 0)), col],
        out_specs=[tok_t, tok_t, tok_t, col],
        scratch_shapes=[pltpu.VMEM((n_e, 1), F32)],
        compiler_params=_cparams(("arbitrary",)),
        name="route",
    )(logits, router_bias.reshape(n_e, 1))


def _slot_rows_kernel(ps_ref, e_ref, p_ref, o_ref, *, n_e):
    e = e_ref[...]
    base = jnp.zeros(e.shape, I32)
    for x in range(n_e):
        base = jnp.where(e == x, ps_ref[x], base)
    o_ref[...] = base + p_ref[...]


def slot_rows(sel_e, sel_pos, pad_starts, tm):
    n = sel_e.shape[1]
    n_e = pad_starts.shape[0]
    tok_t = pl.BlockSpec((SUBLANES, tm), lambda i: (0, i))
    return pl.pallas_call(
        functools.partial(_slot_rows_kernel, n_e=n_e),
        out_shape=jax.ShapeDtypeStruct((SUBLANES, n), I32),
        grid=(n // tm,),
        in_specs=[pl.BlockSpec(memory_space=pltpu.SMEM), tok_t, tok_t],
        out_specs=tok_t,
        compiler_params=_cparams(("arbitrary",)),
        name="slot_rows",
    )(pad_starts.astype(I32), sel_e, sel_pos)


def _row_copy(src_ref, dst_ref, src_row, dst_row, sem):
    return pltpu.make_async_copy(src_ref.at[pl.ds(src_row, 1)], dst_ref.at[pl.ds(dst_row, 1)], sem)


def _dispatch_kernel(dest_ref, h_ref, hs_ref, sem):
    tm = h_ref.shape[0]

    def body(t, carry):
        for kk in range(TOP_K):
            _row_copy(h_ref, hs_ref, t, dest_ref[0, 0, t * TOP_K + kk], sem).start()
        return carry

    def drain(t, carry):
        for kk in range(TOP_K):
            _row_copy(h_ref, hs_ref, t, dest_ref[0, 0, t * TOP_K + kk], sem).wait()
        return carry

    lax.fori_loop(0, tm, body, 0, unroll=4)
    lax.fori_loop(0, tm, drain, 0, unroll=4)


def dispatch(h_all, dest, n_rows, tm):
    n, d = h_all.shape
    assert n % tm == 0
    dest3 = dest.reshape(n // tm, 1, tm * TOP_K)
    return pl.pallas_call(
        _dispatch_kernel,
        out_shape=jax.ShapeDtypeStruct((n_rows, d), h_all.dtype),
        grid=(n // tm,),
        in_specs=[pl.BlockSpec((1, 1, tm * TOP_K), lambda i: (i, 0, 0), memory_space=pltpu.SMEM),
                  pl.BlockSpec((tm, d), lambda i: (i, 0))],
        out_specs=pl.BlockSpec(memory_space=pl.ANY),
        scratch_shapes=[pltpu.SemaphoreType.DMA(())],
        compiler_params=_cparams(("arbitrary",)),
        name="dispatch",
    )(dest3, h_all)


def _experts_kernel(be_ref, bf_ref, bv_ref, x_ref, wg_ref, wu_ref, wd_ref, y_ref, wg_sc, wu_sc, wd_sc):
    i = pl.program_id(0)
    valid = bv_ref[i]

    @pl.when(bf_ref[i] == 1)
    def _():
        wg_sc[...] = wg_ref[0].astype(BF16)
        wu_sc[...] = wu_ref[0].astype(BF16)
        wd_sc[...] = wd_ref[0].astype(BF16)

    @pl.when(valid > 0)
    def _():
        row = lax.broadcasted_iota(I32, x_ref.shape, 0)
        x = jnp.where(row < valid, x_ref[...], jnp.uint32(0))
        y_ref[...] = _pack_pairs(_swiglu_packed(x, wg_sc, wu_sc, wd_sc))

    @pl.when(valid == 0)
    def _():
        y_ref[...] = jnp.zeros_like(y_ref)


def _swiglu_packed(xp, wg_ref, wu_ref, wd_ref):
    lo, hi = _unpack_pairs(xp)
    lo = lo.astype(BF16)
    hi = hi.astype(BF16)
    half = xp.shape[-1]

    def proj(w_ref):
        return (jnp.dot(lo, w_ref[:half, :], preferred_element_type=F32)
                + jnp.dot(hi, w_ref[half:, :], preferred_element_type=F32))

    act = (_silu(proj(wg_ref)) * proj(wu_ref)).astype(BF16)
    return jnp.dot(act, wd_ref[...], preferred_element_type=F32)


def experts(h_sorted, blk_e, blk_first, blk_valid, w_gate, w_up, w_down):
    n_rows, dp = h_sorted.shape
    n_blocks = n_rows // EXPERT_ROWS
    d, de = w_gate.shape[-2:]
    assert d == 2 * dp
    grid_spec = pltpu.PrefetchScalarGridSpec(
        num_scalar_prefetch=3,
        grid=(n_blocks,),
        in_specs=[pl.BlockSpec((EXPERT_ROWS, dp), lambda i, be, bf, nu: (i, 0)),
                  pl.BlockSpec((1, d, de), lambda i, be, bf, nu: (be[i], 0, 0)),
                  pl.BlockSpec((1, d, de), lambda i, be, bf, nu: (be[i], 0, 0)),
                  pl.BlockSpec((1, de, d), lambda i, be, bf, nu: (be[i], 0, 0))],
        out_specs=pl.BlockSpec((EXPERT_ROWS, dp), lambda i, be, bf, nu: (i, 0)),
        scratch_shapes=[pltpu.VMEM((d, de), BF16), pltpu.VMEM((d, de), BF16), pltpu.VMEM((de, d), BF16)],
    )
    return pl.pallas_call(
        _experts_kernel,
        out_shape=jax.ShapeDtypeStruct((n_rows, dp), jnp.uint32),
        grid_spec=grid_spec,
        compiler_params=_cparams(("arbitrary",)),
        name="experts",
    )(blk_e, blk_first, blk_valid, h_sorted, w_gate, w_up, w_down)


def _combine_kernel(dest_ref, x1_ref, h2_ref, gf_ref, w_ref, wg_ref, wu_ref, wd_ref, nw_ref, yr_ref, o_ref,
                    gbuf, sem):
    bt, tl, d = x1_ref.shape
    tm = bt * tl

    def body(t, carry):
        for kk in range(TOP_K):
            _row_copy(yr_ref, gbuf.at[kk], dest_ref[0, 0, t * TOP_K + kk], t, sem).start()
        return carry

    lax.fori_loop(0, tm, body, 0, unroll=4)
    y = _swiglu_packed(h2_ref[...].reshape(tm, d // 2), wg_ref, wu_ref, wd_ref)
    wts = w_ref[...]

    def drain(t, carry):
        for kk in range(TOP_K):
            _row_copy(yr_ref, gbuf.at[kk], dest_ref[0, 0, t * TOP_K + kk], t, sem).wait()
        return carry

    lax.fori_loop(0, tm, drain, 0)
    r_lo = jnp.zeros((tm, d // 2), F32)
    r_hi = jnp.zeros((tm, d // 2), F32)
    for kk in range(TOP_K):
        lo, hi = _unpack_pairs(gbuf[kk])
        r_lo = r_lo + lo * wts[:, kk:kk + 1]
        r_hi = r_hi + hi * wts[:, kk:kk + 1]
    y = y + jnp.concatenate([r_lo, r_hi], axis=1)
    x2 = x1_ref[...] + gf_ref[:, 0] * y.reshape(bt, tl, d)
    o_ref[...] = x2 * lax.rsqrt(jnp.mean(x2 * x2, axis=-1, keepdims=True) + EPS) * nw_ref[...]


def combine(x1, h2, mod4, sel_w, dest, y_rows, w_gs, w_us, w_ds, norm_final, bt, tl):
    b, l, d = x1.shape
    tm = bt * tl
    n = b * l
    ds = w_gs.shape[1]
    n_t = l // tl
    dest3 = dest.reshape(n // tm, 1, tm * TOP_K)

    def const(shape):
        return pl.BlockSpec(shape, lambda i, t: (0, 0), pipeline_mode=pl.Buffered(1))

    return pl.pallas_call(
        _combine_kernel,
        out_shape=jax.ShapeDtypeStruct((b, l, d), F32),
        grid=(b // bt, n_t),
        in_specs=[pl.BlockSpec((1, 1, tm * TOP_K), lambda i, t: (i * n_t + t, 0, 0), memory_space=pltpu.SMEM),
                  pl.BlockSpec((bt, tl, d), lambda i, t: (i, t, 0)),
                  pl.BlockSpec((bt, tl, d // 2), lambda i, t: (i, t, 0)),
                  pl.BlockSpec((bt, 1, 1, d), lambda i, t: (i, 5, 0, 0)),
                  pl.BlockSpec((tm, TOP_K), lambda i, t: (i * n_t + t, 0)),
                  const((d, ds)), const((d, ds)), const((ds, d)), const((1, d)),
                  pl.BlockSpec(memory_space=pl.ANY)],
        out_specs=pl.BlockSpec((bt, tl, d), lambda i, t: (i, t, 0)),
        scratch_shapes=[pltpu.VMEM((TOP_K, tm, d // 2), jnp.uint32), pltpu.SemaphoreType.DMA(())],
        compiler_params=_cparams(("arbitrary", "arbitrary")),
        name="combine",
    )(dest3, x1, h2, mod4, sel_w, w_gs, w_us, w_ds, norm_final.reshape(1, d), y_rows)


def _moe(h2_p, h2_s, x1_p, x1_s, lg_p, lg_s, mod_p, mod_s, router_bias, w_gate_e, w_up_e, w_down_e,
         w_gs, w_us, w_ds, norm_final, tiles_p, tiles_s):
    d = h2_p.shape[-1]
    n_p = h2_p.shape[0] * h2_p.shape[1]
    n_s = h2_s.shape[0] * h2_s.shape[1]
    n = n_p + n_s
    n_e = w_gate_e.shape[0]
    h_all = jnp.concatenate([h2_p.reshape(n_p, d), h2_s.reshape(n_s, d)], axis=0)
    logits = jnp.concatenate([lg_p.reshape(n_p, LANES), lg_s.reshape(n_s, LANES)], axis=0)
    sel_e, sel_w, sel_pos, counts = route(logits, router_bias, n_e, 512)
    counts = counts[:, 0]
    sel_w = sel_w[:TOP_K].T
    padded = (counts + EXPERT_ROWS - 1) // EXPERT_ROWS * EXPERT_ROWS
    pad_ends = jnp.cumsum(padded)
    pad_starts = pad_ends - padded
    n_blocks = -(-(n * TOP_K + n_e * (EXPERT_ROWS - 1)) // EXPERT_ROWS)
    n_rows = n_blocks * EXPERT_ROWS
    dest = slot_rows(sel_e, sel_pos, pad_starts, 512)[:TOP_K].T
    blk_start = jnp.arange(n_blocks, dtype=I32) * EXPERT_ROWS
    n_used = (pad_ends[-1] // EXPERT_ROWS).astype(I32)
    blk_e = jnp.minimum(jnp.sum((blk_start[:, None] >= pad_ends[None, :]).astype(I32), axis=1), n_e - 1)
    blk_valid = jnp.clip((pad_starts + counts)[blk_e] - blk_start, 0, EXPERT_ROWS).astype(I32)
    last_used_e = blk_e[jnp.maximum(n_used - 1, 0)]
    blk_e = jnp.where(jnp.arange(n_blocks) < n_used, blk_e, last_used_e)
    blk_first = jnp.concatenate([jnp.ones((1,), I32), (blk_e[1:] != blk_e[:-1]).astype(I32)])
    h_sorted = dispatch(h_all, dest, n_rows, 512)
    y_rows = experts(h_sorted, blk_e, blk_first, blk_valid, w_gate_e, w_up_e, w_down_e)
    y_p = combine(x1_p, h2_p, mod_p, sel_w[:n_p], dest[:n_p], y_rows, w_gs, w_us, w_ds, norm_final, *tiles_p)
    y_s = combine(x1_s, h2_s, mod_s, sel_w[n_p:], dest[n_p:], y_rows, w_gs, w_us, w_ds, norm_final, *tiles_s)
    return y_p, y_s


def kernel(x_prompt, x_sample, cache_k, cache_v, state_delta, state_conv, page_table, c_prompt, c_sample,
           w_ada, b_ada, norm_mix, w_in, conv_w, a_log, dt_bias, delta_norm, w_branch_a, w_branch_b, w_out,
           rel_bias, norm_ffn, router_w, router_bias, w_gate_e, w_up_e, w_down_e, w_gate_s, w_up_s, w_down_s,
           norm_final):
    depth = w_ada.shape[0]
    assert depth == 1, "the final norm is fused into the layer's last stage"
    n_b, seq, d = x_prompt.shape
    d_b, d_seq, _ = x_sample.shape
    n_pages = page_table.shape[1]
    page = cache_k.shape[2]
    past_len = n_pages * page
    a_width = A_HEADS * HEAD_DIM
    g_width = G_HEADS * HEAD_DIM
    conv_ch = 3 * g_width
    l = 0

    off_small = 3 * a_width + conv_ch + g_width
    w_l = w_in[l]
    w_main = jnp.concatenate([w_l[:, :off_small].astype(BF16), w_l[:, off_small + 2 * G_HEADS:].astype(BF16)],
                             axis=1)
    w_small = jnp.pad(w_l[:, off_small:off_small + 2 * G_HEADS], ((0, 0), (0, LANES - 2 * G_HEADS))).astype(BF16)
    assert w_main.shape[1] == IN_COL_TILES * IN_TILE

    c_all = jnp.concatenate([c_prompt, c_sample], axis=0)
    mod = ada_mod(c_all, w_ada[l], b_ada[l])
    mod_p = mod[:n_b].reshape(n_b, N_ADA, 1, d)
    mod_s = mod[n_b:].reshape(d_b, N_ADA, 1, d)

    tiles_p = (1, 512)
    tiles_s = (512 // d_seq, d_seq)
    post_p = (1, 256)
    post_s = (256 // d_seq, d_seq)
    wa = w_branch_a[l].astype(BF16)
    wb = w_branch_b[l].astype(BF16)
    wo = w_out[l].astype(BF16)
    rw = jnp.pad(router_w[l], ((0, 0), (0, LANES - router_w.shape[-1])))

    def mixer(x, mod4, tiles, post_tiles, attend, s0, buf0, chunk):
        a_q, a_k, a_v, g_qkv, g_z, gate_a, gate_b, small = in_proj(x, mod4, norm_mix[l], w_main, w_small, *tiles)
        y_a = attend(a_q, a_k, a_v)
        small_t = jnp.swapaxes(small[..., :2 * G_HEADS], 1, 2)
        y_b, s_new, tail = delta_rule(g_qkv, g_z, small, small_t, conv_w[l], a_log[l], dt_bias[l],
                                      delta_norm[l], s0, buf0, chunk)
        x1, h2, lg = post_mix(y_a, y_b, gate_a, gate_b, x, mod4, wa, wb, wo, norm_ffn[l], rw, *post_tiles)
        return x1, h2, lg, a_k, a_v, s_new, tail[:, SUBLANES - (CONV_W - 1):]

    s0_p = jnp.zeros((n_b, G_HEADS, HEAD_DIM, HEAD_DIM), F32)
    buf0_p = jnp.zeros((n_b, SUBLANES, conv_ch), F32)
    buf0_s = jnp.pad(state_conv[l], ((0, 0), (SUBLANES - (CONV_W - 1), 0), (0, 0)))

    x1_p, h2_p, lg_p, k_p, v_p, d_p, cv_p = mixer(
        x_prompt, mod_p, tiles_p, post_p, functools.partial(moba_prompt, rel_bias=rel_bias), s0_p, buf0_p,
        DELTA_CHUNK)
    x1_s, h2_s, lg_s, k_s, v_s, d_s, cv_s = mixer(
        x_sample, mod_s, tiles_s, post_s,
        functools.partial(moba_sample, cache_k=cache_k, cache_v=cache_v, layer=l, page_table=page_table,
                          rel_bias=rel_bias, past_len=past_len),
        state_delta[l], buf0_s, d_seq)

    comb_p = (1, 256)
    comb_s = (256 // d_seq, d_seq)
    y_p, y_s = _moe(h2_p, h2_s, x1_p, x1_s, lg_p, lg_s, mod_p, mod_s, router_bias[l],
                    w_gate_e[l], w_up_e[l], w_down_e[l],
                    w_gate_s[l].astype(BF16), w_up_s[l].astype(BF16), w_down_s[l].astype(BF16),
                    norm_final, comb_p, comb_s)

    def heads(t, n_heads):
        return t.reshape(t.shape[:-1] + (n_heads, HEAD_DIM))[None]

    return (y_p, y_s, heads(k_p, A_HEADS), heads(v_p, A_HEADS), heads(k_s, A_HEADS), heads(v_s, A_HEADS),
            d_p[None], cv_p[None], d_s[None], cv_s[None])
```

```python
import functools
import math

import numpy as np
import jax
import jax.numpy as jnp
from jax import lax
from jax.experimental import pallas as pl
from jax.experimental.pallas import tpu as pltpu

F32 = jnp.float32
BF16 = jnp.bfloat16
I32 = jnp.int32
EPS = 1e-6
NEG_INF = float("-inf")

A_HEADS = 8
G_HEADS = 8
MOBA_BLOCK = 256
MOBA_TOPK = 3
REL_BUCKETS = 32
REL_MAX_DIST = 128
CONV_W = 4
N_GROUPS = 8
TOPK_GROUPS = 4
TOP_K = 6
ROUTED_SCALE = 2.5
N_ADA = 6
HEAD_DIM = 128

LANES = 128
SUBLANES = 8
VMEM_LIMIT = 56 * 1024 * 1024
EXPERT_ROWS = 256
DELTA_CHUNK = 128
SAMPLE_BLOCKS = 8


def _cparams(sem, vmem=VMEM_LIMIT):
    return pltpu.CompilerParams(dimension_semantics=sem, vmem_limit_bytes=vmem)


def _dot(a, b):
    return jnp.dot(a.astype(BF16), b.astype(BF16), preferred_element_type=F32)


def _dot_nt(a, b):
    return lax.dot_general(a.astype(BF16), b.astype(BF16), (((1,), (1,)), ((), ())),
                           preferred_element_type=F32)


def _dot_tn(a, b):
    return lax.dot_general(a.astype(BF16), b.astype(BF16), (((0,), (0,)), ((), ())),
                           preferred_element_type=F32)


def _dot_exact(a, b):
    return jnp.dot(a, b, precision=lax.Precision.HIGHEST, preferred_element_type=F32)


_DIMS = {"nn": (((1,), (0,)), ((), ())), "nt": (((1,), (1,)), ((), ())), "tn": (((0,), (0,)), ((), ()))}

def _mm(a, b, form, passes):
    dims = _DIMS[form]
    a_hi = a.astype(BF16)
    b_hi = b.astype(BF16)
    out = lax.dot_general(a_hi, b_hi, dims, preferred_element_type=F32)
    if passes == 3:
        a_lo = (a - a_hi.astype(F32)).astype(BF16)
        b_lo = (b - b_hi.astype(F32)).astype(BF16)
        out = out + (lax.dot_general(a_hi, b_lo, dims, preferred_element_type=F32)
                     + lax.dot_general(a_lo, b_hi, dims, preferred_element_type=F32))
    return out


def _pack_pairs(x):
    w = x.shape[-1] // 2
    lo = lax.bitcast_convert_type(x[:, :w].astype(BF16).astype(F32), jnp.uint32)
    hi = lax.bitcast_convert_type(x[:, w:].astype(BF16).astype(F32), jnp.uint32)
    return (lo >> 16) | (hi & jnp.uint32(0xFFFF0000))


def _unpack_pairs(u):
    lo = lax.bitcast_convert_type(u << 16, F32)
    hi = lax.bitcast_convert_type(u & jnp.uint32(0xFFFF0000), F32)
    return lo, hi


def _silu(x):
    return x * jax.nn.sigmoid(x)


def _rel_bucket_np(dist):
    n = np.maximum(dist, 0)
    max_exact = REL_BUCKETS // 2
    nf = np.maximum(n, 1).astype(np.float32)
    large = max_exact + (np.log(nf / np.float32(max_exact)) / np.float32(math.log(REL_MAX_DIST / max_exact))
                         * np.float32(REL_BUCKETS - max_exact)).astype(np.int32)
    return np.where(n < max_exact, n, np.minimum(large, REL_BUCKETS - 1)).astype(np.int32)


def _ada_kernel(c_ref, w_ref, b_ref, o_ref):
    o_ref[...] = _dot(_silu(c_ref[...]), w_ref[...]) + b_ref[...]


def ada_mod(c_all, w_ada, b_ada):
    rows, d = c_all.shape
    n = w_ada.shape[1]
    tn = 1024
    return pl.pallas_call(
        _ada_kernel,
        out_shape=jax.ShapeDtypeStruct((rows, n), F32),
        grid=(n // tn,),
        in_specs=[pl.BlockSpec((rows, d), lambda j: (0, 0)),
                  pl.BlockSpec((d, tn), lambda j: (0, j)),
                  pl.BlockSpec((1, tn), lambda j: (0, j))],
        out_specs=pl.BlockSpec((rows, tn), lambda j: (0, j)),
        compiler_params=_cparams(("arbitrary",)),
        name="ada_mod",
    )(c_all, w_ada, b_ada.reshape(1, n))


IN_TILE = 1024
IN_GROUPS = (("a_q", 0, 1), ("a_k", 1, 1), ("a_v", 2, 1), ("g_qkv", 3, 3), ("g_z", 6, 1),
             ("gate_a", 7, 2), ("gate_b", 9, 2))
IN_COL_TILES = 11


def _inproj_kernel(x_ref, sc_ref, sh_ref, nw_ref, w_ref, ws_ref, *rest):
    out_refs = rest[:len(IN_GROUPS)]
    small_ref = rest[len(IN_GROUPS)]
    h_ref = rest[len(IN_GROUPS) + 1]
    bt, tl, d = x_ref.shape
    j = pl.program_id(2)

    @pl.when(j == 0)
    def _():
        x = x_ref[...]
        y = x * lax.rsqrt(jnp.mean(x * x, axis=-1, keepdims=True) + EPS) * nw_ref[...]
        h = y * (1.0 + sc_ref[:, 0]) + sh_ref[:, 0]
        h2 = h.reshape(bt * tl, d).astype(BF16)
        h_ref[...] = h2
        small_ref[...] = jnp.dot(h2, ws_ref[...], preferred_element_type=F32).reshape(bt, tl, LANES)

    res = jnp.dot(h_ref[...], w_ref[...], preferred_element_type=F32).reshape(bt, tl, IN_TILE)
    for o_ref, (_, j0, nj) in zip(out_refs, IN_GROUPS):
        @pl.when((j >= j0) & (j < j0 + nj))
        def _(o_ref=o_ref):
            o_ref[...] = res


def in_proj(x, mod4, norm_w, w_main, w_small, bt, tl):
    b, l, d = x.shape
    grid = (b // bt, l // tl, IN_COL_TILES)
    out_shapes, out_specs = [], []
    for _, j0, nj in IN_GROUPS:
        out_shapes.append(jax.ShapeDtypeStruct((b, l, nj * IN_TILE), F32))
        out_specs.append(pl.BlockSpec((bt, tl, IN_TILE),
                                      lambda i, t, j, j0=j0, nj=nj: (i, t, jnp.clip(j - j0, 0, nj - 1))))
    out_shapes.append(jax.ShapeDtypeStruct((b, l, LANES), F32))
    out_specs.append(pl.BlockSpec((bt, tl, LANES), lambda i, t, j: (i, t, 0)))
    return pl.pallas_call(
        _inproj_kernel,
        out_shape=out_shapes,
        grid=grid,
        in_specs=[pl.BlockSpec((bt, tl, d), lambda i, t, j: (i, t, 0)),
                  pl.BlockSpec((bt, 1, 1, d), lambda i, t, j: (i, 1, 0, 0)),
                  pl.BlockSpec((bt, 1, 1, d), lambda i, t, j: (i, 0, 0, 0)),
                  pl.BlockSpec((1, d), lambda i, t, j: (0, 0)),
                  pl.BlockSpec((d, IN_TILE), lambda i, t, j: (0, j)),
                  pl.BlockSpec((d, LANES), lambda i, t, j: (0, 0))],
        out_specs=out_specs,
        scratch_shapes=[pltpu.VMEM((bt * tl, d), BF16)],
        compiler_params=_cparams(("arbitrary", "arbitrary", "arbitrary")),
        name="in_proj",
    )(x, mod4, mod4, norm_w.reshape(1, d), w_main, w_small)


def _softmax_step(s, v_blk, m, l, acc):
    m_new = jnp.maximum(m, jnp.max(s, axis=-1, keepdims=True))
    alpha = jnp.exp(m - m_new)
    p = jnp.exp(s - m_new)
    l = alpha * l + jnp.sum(p, axis=-1, keepdims=True)
    acc = alpha * acc + _dot(p, v_blk)
    return m_new, l, acc


def _topk_select(cols, k):
    sels = []
    for n, gn in enumerate(cols):
        rank = jnp.zeros(gn.shape, F32)
        for m_, gm in enumerate(cols):
            if m_ == n:
                continue
            ahead = (gm >= gn) if m_ < n else (gm > gn)
            rank = rank + jnp.where(ahead, 1.0, 0.0)
        sels.append(rank < float(k))
    return sels


def _moba_prompt_kernel(rb_ref, bkt_ref, q_ref, k_ref, v_ref, o_ref, bias_ref):
    h = pl.program_id(0)
    s_len = q_ref.shape[1]
    blk = MOBA_BLOCK
    nb = s_len // blk
    scale = HEAD_DIM ** -0.5

    @pl.when(pl.program_id(1) == 0)
    def _():
        row = lax.broadcasted_iota(I32, (blk, blk), 0)
        col = lax.broadcasted_iota(I32, (blk, blk), 1)
        for t in range(2):
            bkt = bkt_ref[t]
            bias = jnp.zeros((blk, blk), F32)
            for r in range(REL_BUCKETS):
                bias = jnp.where(bkt == r, rb_ref[r, h], bias)
            bias_ref[t] = jnp.where(col <= row, bias, NEG_INF) if t == 0 else bias

    bias_far = rb_ref[REL_BUCKETS - 1, h]
    k_means = [jnp.mean(k_ref[0, n * blk:(n + 1) * blk, :], axis=0, keepdims=True) for n in range(nb)]

    for qb in range(nb):
        q = q_ref[0, qb * blk:(qb + 1) * blk, :]
        qs = (q * scale).astype(BF16)
        if qb > MOBA_TOPK:
            gates = [jnp.sum(q * k_means[n], axis=-1, keepdims=True) for n in range(qb)]
            masks = [jnp.where(sel, 0.0, NEG_INF) for sel in _topk_select(gates, MOBA_TOPK)]
        else:
            masks = [None] * qb
        s = _dot_nt(qs, k_ref[0, qb * blk:(qb + 1) * blk, :]) + bias_ref[0]
        m = jnp.max(s, axis=-1, keepdims=True)
        p = jnp.exp(s - m)
        l = jnp.sum(p, axis=-1, keepdims=True)
        acc = _dot(p, v_ref[0, qb * blk:(qb + 1) * blk, :])
        for n in range(qb - 1, -1, -1):
            s = _dot_nt(qs, k_ref[0, n * blk:(n + 1) * blk, :])
            if n == qb - 1:
                s = s + bias_ref[1]
                if masks[n] is not None:
                    s = s + masks[n]
            else:
                s = s + (bias_far if masks[n] is None else masks[n] + bias_far)
            m, l, acc = _softmax_step(s, v_ref[0, n * blk:(n + 1) * blk, :], m, l, acc)
        o_ref[0, qb * blk:(qb + 1) * blk, :] = acc / l


def moba_prompt(q, k, v, rel_bias):
    b, s_len, width = q.shape
    assert s_len % MOBA_BLOCK == 0 and width == A_HEADS * HEAD_DIM
    ar = np.arange(MOBA_BLOCK)
    d_loc = ar[:, None] - ar[None, :]
    bkt = np.stack([_rel_bucket_np(d_loc), _rel_bucket_np(d_loc + MOBA_BLOCK)]).astype(np.int32)
    assert int(_rel_bucket_np(np.array([MOBA_BLOCK + 1]))[0]) == REL_BUCKETS - 1
    spec = pl.BlockSpec((1, s_len, HEAD_DIM), lambda h, i: (i, 0, h))
    return pl.pallas_call(
        _moba_prompt_kernel,
        out_shape=jax.ShapeDtypeStruct((b, s_len, width), F32),
        grid=(A_HEADS, b),
        in_specs=[pl.BlockSpec(memory_space=pltpu.SMEM),
                  pl.BlockSpec((2, MOBA_BLOCK, MOBA_BLOCK), lambda h, i: (0, 0, 0)),
                  spec, spec, spec],
        out_specs=spec,
        scratch_shapes=[pltpu.VMEM((2, MOBA_BLOCK, MOBA_BLOCK), F32)],
        compiler_params=_cparams(("arbitrary", "arbitrary")),
        name="moba_prompt",
    )(rel_bias, jnp.asarray(bkt), q, k, v)


def _moba_sample_kernel(pt_ref, rb_ref, bkt_ref, q_ref, kn_ref, vn_ref, *rest, past_len):
    nh = A_HEADS
    n_pg = 2 * SAMPLE_BLOCKS
    k_pages_all = rest[0:n_pg]
    v_pages_all = rest[n_pg:2 * n_pg]
    o_ref, bias_ref, m_ref, l_ref, g_ref, acc_ref = rest[2 * n_pg:]
    page = k_pages_all[0].shape[1] // nh

    def head_rows(ref, h):
        return ref[0, pl.ds(h, page, stride=nh), :]

    n = pl.program_id(1)
    nb = pl.num_programs(1)
    t = q_ref.shape[1]
    dh = HEAD_DIM
    blk = MOBA_BLOCK
    scale = dh ** -0.5
    n_last = past_len // blk - 1

    @pl.when(n == 0)
    def _():
        for h in range(nh):
            bkt = bkt_ref[...]
            bias = jnp.zeros((t, blk), F32)
            for r in range(REL_BUCKETS):
                bias = jnp.where(bkt == r, rb_ref[r, h], bias)
            bias_ref[h * t:(h + 1) * t, :] = bias

    q_all = q_ref[0]
    qhs = [q_all[:, h * dh:(h + 1) * dh] for h in range(nh)]
    qss = [(qh * scale).astype(BF16) for qh in qhs]
    heads = range(nh)
    for jb in range(SAMPLE_BLOCKS):
        gi = n * SAMPLE_BLOCKS + jb
        k_pages = k_pages_all[2 * jb:2 * jb + 2]
        v_pages = v_pages_all[2 * jb:2 * jb + 2]
        k_blks = [jnp.concatenate([head_rows(k_pages[0], h), head_rows(k_pages[1], h)], axis=0) for h in heads]
        v_blks = [jnp.concatenate([head_rows(v_pages[0], h), head_rows(v_pages[1], h)], axis=0) for h in heads]
        ss = [_dot_nt(qss[h], k_blks[h]) for h in heads]
        gs = [jnp.sum(qhs[h] * jnp.mean(k_blks[h], axis=0, keepdims=True), axis=-1, keepdims=True)
              for h in heads]
        ss = [ss[h] + jnp.where(gi == n_last, bias_ref[h * t:(h + 1) * t, :], rb_ref[REL_BUCKETS - 1, h])
              for h in heads]
        ms = [jnp.max(s, axis=-1, keepdims=True) for s in ss]
        ps = [jnp.exp(s - m) for s, m in zip(ss, ms)]
        accs = [_dot(ps[h], v_blks[h]) for h in heads]
        m_ref[gi] = jnp.concatenate(ms, axis=0)
        l_ref[gi] = jnp.concatenate([jnp.sum(p, axis=-1, keepdims=True) for p in ps], axis=0)
        g_ref[gi] = jnp.concatenate(gs, axis=0)
        acc_ref[gi] = jnp.concatenate(accs, axis=0)

    @pl.when(n == nb - 1)
    def _():
        n_blocks = past_len // blk
        rows = nh * t
        sels = _topk_select([g_ref[i] for i in range(n_blocks)], MOBA_TOPK)
        tq = lax.broadcasted_iota(I32, (t, t), 0)
        tk = lax.broadcasted_iota(I32, (t, t), 1)
        s_locs = []
        for h in range(nh):
            bias_loc = jnp.zeros((t, t), F32)
            for d in range(t):
                bias_loc = jnp.where(tq - tk == d, rb_ref[d, h], bias_loc)
            s_loc = _dot_nt(qss[h], kn_ref[0, :, h * dh:(h + 1) * dh]) + bias_loc
            s_locs.append(jnp.where(tk <= tq, s_loc, NEG_INF))
        s_loc = jnp.concatenate(s_locs, axis=0)
        m_tot = jnp.max(s_loc, axis=-1, keepdims=True)
        for i in range(n_blocks):
            m_tot = jnp.maximum(m_tot, jnp.where(sels[i], m_ref[i], NEG_INF))
        p_loc = jnp.exp(s_loc - m_tot)
        l_tot = jnp.sum(p_loc, axis=-1, keepdims=True)
        acc = jnp.concatenate([_dot(p_loc[h * t:(h + 1) * t, :], vn_ref[0, :, h * dh:(h + 1) * dh])
                               for h in range(nh)], axis=0)
        for i in range(n_blocks):
            w = jnp.where(sels[i], jnp.exp(jnp.where(sels[i], m_ref[i] - m_tot, 0.0)), 0.0)
            l_tot = l_tot + w * l_ref[i]
            acc = acc + w * acc_ref[i]
        out = acc / l_tot
        o_ref[0] = jnp.concatenate([out[h * t:(h + 1) * t, :] for h in range(nh)], axis=1)


def moba_sample(q, k_new, v_new, cache_k, cache_v, layer, page_table, rel_bias, past_len):
    db, t, width = q.shape
    page = cache_k.shape[2]
    assert MOBA_BLOCK == 2 * page and past_len % MOBA_BLOCK == 0 and t <= REL_BUCKETS // 2
    assert past_len // MOBA_BLOCK >= MOBA_TOPK and t % SUBLANES == 0
    n_blocks = past_len // MOBA_BLOCK
    assert n_blocks % SAMPLE_BLOCKS == 0
    rows = A_HEADS * t
    d_last = MOBA_BLOCK + np.arange(t)[:, None] - np.arange(MOBA_BLOCK)[None, :]
    bkt = _rel_bucket_np(d_last).astype(np.int32)
    assert int(_rel_bucket_np(np.array([MOBA_BLOCK + 1]))[0]) == REL_BUCKETS - 1
    tok_spec = pl.BlockSpec((1, t, width), lambda i, n, pt: (i, 0, 0))

    n_phys = cache_k.shape[1]
    ck = cache_k.reshape(cache_k.shape[0] * n_phys, page * A_HEADS, HEAD_DIM)
    cv = cache_v.reshape(cache_v.shape[0] * n_phys, page * A_HEADS, HEAD_DIM)

    n_pg = 2 * SAMPLE_BLOCKS
    page_specs = [pl.BlockSpec((1, page * A_HEADS, HEAD_DIM),
                               lambda i, n, pt, j=j: (layer * n_phys + pt[i, n_pg * n + j], 0, 0))
                  for j in range(n_pg)]

    grid_spec = pltpu.PrefetchScalarGridSpec(
        num_scalar_prefetch=1,
        grid=(db, n_blocks // SAMPLE_BLOCKS),
        in_specs=[pl.BlockSpec(memory_space=pltpu.SMEM),
                  pl.BlockSpec((t, MOBA_BLOCK), lambda i, n, pt: (0, 0)),
                  tok_spec, tok_spec, tok_spec] + page_specs + page_specs,
        out_specs=tok_spec,
        scratch_shapes=[pltpu.VMEM((rows, MOBA_BLOCK), F32),
                        pltpu.VMEM((n_blocks, rows, 1), F32),
                        pltpu.VMEM((n_blocks, rows, 1), F32),
                        pltpu.VMEM((n_blocks, rows, 1), F32),
                        pltpu.VMEM((n_blocks, rows, HEAD_DIM), F32)],
    )
    return pl.pallas_call(
        functools.partial(_moba_sample_kernel, past_len=past_len),
        out_shape=jax.ShapeDtypeStruct((db, t, width), F32),
        grid_spec=grid_spec,
        compiler_params=_cparams(("arbitrary", "arbitrary")),
        name="moba_sample",
    )(page_table, rel_bias, jnp.asarray(bkt), q, k_new, v_new, *([ck] * n_pg + [cv] * n_pg))


INV_BASE = 16


def _unit_lower_inverses(a_mats, ri, ci, size):
    base = min(INV_BASE, size)
    same = ri // base == ci // base
    eye = jnp.where(ri == ci, 1.0, 0.0)
    pws = [jnp.where(same, a, 0.0) for a in a_mats]
    invs = [eye - p for p in pws]
    for _ in range(max(int(math.log2(base)) - 1, 0)):
        pws = [_mm(p, p, "nn", 1) for p in pws]
        invs = [i + _mm(i, p, "nn", 1) for i, p in zip(invs, pws)]
    blk = base
    while blk < size:
        pr = ri // blk
        pc = ci // blk
        join = (pr == pc + 1) & (pr // 2 == pc // 2)
        halves = [_mm(i, jnp.where(join, a, 0.0), "nn", 1) for i, a in zip(invs, a_mats)]
        invs = [i - _mm(hf, i, "nn", 1) for i, hf in zip(invs, halves)]
        blk *= 2
    return invs


def _delta_kernel(x_ref, cw_ref, z_ref, sm_ref, smt_ref, ar_ref, dr_ref, ac_ref, dc_ref, nw_ref, s0_ref, b0_ref,
                  y_ref, sn_ref, tail_ref, s_sc, carry_sc):
    c = pl.program_id(1)
    nc = pl.num_programs(1)
    chunk = x_ref.shape[1]
    dk = HEAD_DIM
    hh = G_HEADS
    part = hh * dk

    @pl.when(c == 0)
    def _():
        s_sc[...] = s0_ref[0]
        carry_sc[...] = b0_ref[0]

    x = x_ref[0]
    xx = jnp.concatenate([carry_sc[...], x], axis=0)
    cw = cw_ref[...]
    y = x * cw[CONV_W - 1:CONV_W, :]
    for s in range(1, CONV_W):
        y = y + xx[SUBLANES - s:SUBLANES - s + chunk, :] * cw[CONV_W - 1 - s:CONV_W - s, :]
    carry_sc[...] = xx[chunk:chunk + SUBLANES, :]
    qkv = _silu(y)

    def softplus(t):
        return jnp.maximum(t, 0.0) + jnp.log(1.0 + jnp.exp(-jnp.abs(t)))

    sm = sm_ref[0]
    beta_all = jax.nn.sigmoid(sm)
    g_cols = -jnp.exp(ar_ref[...]) * softplus(sm + dr_ref[...])
    g_rows = -jnp.exp(ac_ref[...]) * softplus(smt_ref[0] + dc_ref[...])
    ri = lax.broadcasted_iota(I32, (chunk, chunk), 0)
    ci = lax.broadcasted_iota(I32, (chunk, chunk), 1)
    incl = ci <= ri
    strict = ci < ri
    gc_cols = _dot_exact(jnp.where(incl, 1.0, 0.0), g_cols)
    gc_rows = _dot_exact(g_rows, jnp.where(ri <= ci, 1.0, 0.0))
    nw = nw_ref[...]

    heads = range(hh)
    qs, ks, vs, betas, gcs, dmasks = [], [], [], [], [], []
    for h in heads:
        q = qkv[:, h * dk:(h + 1) * dk]
        k = qkv[:, part + h * dk:part + (h + 1) * dk]
        qs.append(q * lax.rsqrt(jnp.sum(q * q, axis=-1, keepdims=True) + EPS) * (dk ** -0.5))
        ks.append(k * lax.rsqrt(jnp.sum(k * k, axis=-1, keepdims=True) + EPS))
        vs.append(qkv[:, 2 * part + h * dk:2 * part + (h + 1) * dk])
        betas.append(beta_all[:, h:h + 1])
        gc_c = gc_cols[:, hh + h:hh + h + 1]
        gc_r = gc_rows[hh + h:hh + h + 1, :]
        gcs.append(gc_c)
        dmasks.append(jnp.exp(jnp.where(incl, gc_c - gc_r, NEG_INF)))

    kbs = [k * b for k, b in zip(ks, betas)]
    a_mats = [jnp.where(strict, _mm(kb, k, "nt", 1) * dm, 0.0) for kb, k, dm in zip(kbs, ks, dmasks)]
    t_invs = _unit_lower_inverses(a_mats, ri, ci, chunk)
    egs = [jnp.exp(g) for g in gcs]
    uws = [_mm(t, jnp.concatenate([v * b, kb * eg], axis=1), "nn", 1)
           for t, v, b, kb, eg in zip(t_invs, vs, betas, kbs, egs)]
    s_mats = [s_sc[h] for h in heads]
    v_news = [uw[:, :dk] - _mm(uw[:, dk:], s, "nn", 1) for uw, s in zip(uws, s_mats)]
    attns = [_mm(q, k, "nt", 1) * dm for q, k, dm in zip(qs, ks, dmasks)]
    outs = [_mm(q * eg, s, "nn", 1) + _mm(at, vn, "nn", 1)
            for q, eg, s, at, vn in zip(qs, egs, s_mats, attns, v_news)]
    for h in heads:
        g_last = gcs[h][chunk - 1:chunk, :]
        s_sc[h] = s_mats[h] * jnp.exp(g_last) + _mm(ks[h] * jnp.exp(g_last - gcs[h]), v_news[h], "tn", 1)
    for h in heads:
        o = outs[h]
        o_n = o * lax.rsqrt(jnp.mean(o * o, axis=-1, keepdims=True) + EPS) * nw
        y_ref[0, :, h * dk:(h + 1) * dk] = o_n * _silu(z_ref[0, :, h * dk:(h + 1) * dk])

    @pl.when(c == nc - 1)
    def _():
        sn_ref[0] = s_sc[...]
        tail_ref[0] = carry_sc[...]


def delta_rule(xg, z, small, small_t, conv_w, a_log, dt_bias, delta_norm, s0, buf0, chunk):
    b, l, cw_ch = xg.shape
    hh = G_HEADS
    assert l % chunk == 0 and cw_ch == 3 * hh * HEAD_DIM and l >= CONV_W - 1
    nc = l // chunk
    width = hh * HEAD_DIM
    a_row = jnp.pad(a_log.reshape(1, hh), ((0, 0), (hh, LANES - 2 * hh)))
    d_row = jnp.pad(dt_bias.reshape(1, hh), ((0, 0), (hh, LANES - 2 * hh)))
    a_col = jnp.pad(a_log.reshape(hh, 1), ((hh, 0), (0, 0)))
    d_col = jnp.pad(dt_bias.reshape(hh, 1), ((hh, 0), (0, 0)))

    def const(shape):
        return pl.BlockSpec(shape, lambda i, c: (0,) * len(shape))

    return pl.pallas_call(
        _delta_kernel,
        out_shape=[jax.ShapeDtypeStruct((b, l, width), F32),
                   jax.ShapeDtypeStruct((b, hh, HEAD_DIM, HEAD_DIM), F32),
                   jax.ShapeDtypeStruct((b, SUBLANES, cw_ch), F32)],
        grid=(b, nc),
        in_specs=[pl.BlockSpec((1, chunk, cw_ch), lambda i, c: (i, c, 0)),
                  const((CONV_W, cw_ch)),
                  pl.BlockSpec((1, chunk, width), lambda i, c: (i, c, 0)),
                  pl.BlockSpec((1, chunk, LANES), lambda i, c: (i, c, 0)),
                  pl.BlockSpec((1, 2 * hh, chunk), lambda i, c: (i, 0, c)),
                  const((1, LANES)), const((1, LANES)), const((2 * hh, 1)), const((2 * hh, 1)),
                  const((1, HEAD_DIM)),
                  pl.BlockSpec((1, hh, HEAD_DIM, HEAD_DIM), lambda i, c: (i, 0, 0, 0)),
                  pl.BlockSpec((1, SUBLANES, cw_ch), lambda i, c: (i, 0, 0))],
        out_specs=[pl.BlockSpec((1, chunk, width), lambda i, c: (i, c, 0)),
                   pl.BlockSpec((1, hh, HEAD_DIM, HEAD_DIM), lambda i, c: (i, 0, 0, 0)),
                   pl.BlockSpec((1, SUBLANES, cw_ch), lambda i, c: (i, 0, 0))],
        scratch_shapes=[pltpu.VMEM((hh, HEAD_DIM, HEAD_DIM), F32),
                        pltpu.VMEM((SUBLANES, cw_ch), F32)],
        compiler_params=_cparams(("arbitrary", "arbitrary")),
        name="delta_rule",
    )(xg, conv_w, z, small, small_t, a_row, d_row, a_col, d_col, delta_norm.reshape(1, HEAD_DIM), s0, buf0)


def _post_kernel(ya_ref, yb_ref, ga_ref, gb_ref, x_ref, gm_ref, scf_ref, shf_ref, wa_ref, wb_ref, wo_ref,
                 nw_ref, rw_ref, x1_ref, h2_ref, lg_ref):
    bt, tl, d = x_ref.shape
    rows = bt * tl
    ya = ya_ref[...].reshape(rows, -1)
    yb = yb_ref[...].reshape(rows, -1)
    merged = (jax.nn.sigmoid(ga_ref[...].reshape(rows, d)) * _dot(ya, wa_ref[...])
              + jax.nn.sigmoid(gb_ref[...].reshape(rows, d)) * _dot(yb, wb_ref[...]))
    mix = _dot(merged, wo_ref[...]).reshape(bt, tl, d)
    x1 = x_ref[...] + gm_ref[:, 0] * mix
    x1_ref[...] = x1
    y = x1 * lax.rsqrt(jnp.mean(x1 * x1, axis=-1, keepdims=True) + EPS) * nw_ref[...]
    h2 = (y * (1.0 + scf_ref[:, 0]) + shf_ref[:, 0]).reshape(rows, d)
    h2_ref[...] = _pack_pairs(h2).reshape(bt, tl, d // 2)
    lg_ref[...] = _mm(h2, rw_ref[...], "nn", 3).reshape(bt, tl, -1)


def post_mix(y_a, y_b, gate_a, gate_b, x, mod4, w_a, w_b, w_o, norm_w, router_w, bt, tl):
    b, l, d = x.shape
    wa_in = y_a.shape[-1]
    wb_in = y_b.shape[-1]
    n_e = router_w.shape[1]

    def tok(width):
        return pl.BlockSpec((bt, tl, width), lambda i, t: (i, t, 0))

    def modspec(idx):
        return pl.BlockSpec((bt, 1, 1, d), lambda i, t, idx=idx: (i, idx, 0, 0))

    def const(shape):
        return pl.BlockSpec(shape, lambda i, t: (0, 0), pipeline_mode=pl.Buffered(1))

    return pl.pallas_call(
        _post_kernel,
        out_shape=[jax.ShapeDtypeStruct((b, l, d), F32), jax.ShapeDtypeStruct((b, l, d // 2), jnp.uint32),
                   jax.ShapeDtypeStruct((b, l, n_e), F32)],
        grid=(b // bt, l // tl),
        in_specs=[tok(wa_in), tok(wb_in), tok(d), tok(d), tok(d), modspec(2), modspec(4), modspec(3),
                  const((wa_in, d)), const((wb_in, d)), const((d, d)), const((1, d)), const((d, n_e))],
        out_specs=[tok(d), tok(d // 2), tok(n_e)],
        compiler_params=_cparams(("arbitrary", "arbitrary")),
        name="post_mix",
    )(y_a, y_b, gate_a, gate_b, x, mod4, mod4, mod4, w_a, w_b, w_o, norm_w.reshape(1, d), router_w)


def _route_kernel(lg_ref, rb_ref, e_ref, w_ref, p_ref, cnt_ref, run_sc, *, n_e):
    i = pl.program_id(0)
    tm = lg_ref.shape[0]
    per_group = n_e // N_GROUPS
    assert per_group == SUBLANES

    @pl.when(i == 0)
    def _():
        run_sc[...] = jnp.zeros_like(run_sc)

    scores = jax.nn.sigmoid(lg_ref[...].T[:n_e, :])
    biased = scores + rb_ref[...]
    sub = lax.broadcasted_iota(I32, (per_group, tm), 0)

    group_scores = []
    for g in range(N_GROUPS):
        xg = biased[g * per_group:(g + 1) * per_group, :]
        m1 = jnp.max(xg, axis=0, keepdims=True)
        i1 = jnp.min(jnp.where(xg == m1, sub, per_group), axis=0, keepdims=True)
        m2 = jnp.max(jnp.where(sub == i1, NEG_INF, xg), axis=0, keepdims=True)
        group_scores.append(m1 + m2)
    keep = _topk_select(group_scores, TOPK_GROUPS)
    masked = jnp.concatenate(
        [jnp.where(keep[g], biased[g * per_group:(g + 1) * per_group, :], NEG_INF) for g in range(N_GROUPS)], axis=0)

    eidx = lax.broadcasted_iota(I32, (n_e, tm), 0)
    sel = jnp.zeros((n_e, tm), jnp.bool_)
    hits = []
    idxs = []
    for _ in range(TOP_K):
        mx = jnp.max(masked, axis=0, keepdims=True)
        idx = jnp.min(jnp.where(masked == mx, eidx, n_e), axis=0, keepdims=True)
        hit = eidx == idx
        sel = sel | hit
        masked = jnp.where(hit, NEG_INF, masked)
        hits.append(hit)
        idxs.append(idx)
    sel_f = jnp.where(sel, 1.0, 0.0)
    top_w = scores * sel_f
    top_w = top_w / jnp.sum(top_w, axis=0, keepdims=True) * ROUTED_SCALE

    ri = lax.broadcasted_iota(I32, (tm, tm), 0)
    ci = lax.broadcasted_iota(I32, (tm, tm), 1)
    before = _dot(sel_f, jnp.where(ri < ci, 1.0, 0.0)) + run_sc[...]
    run_sc[...] = run_sc[...] + jnp.sum(sel_f, axis=1, keepdims=True)
    cnt_ref[...] = run_sc[...].astype(I32)

    pad_rows = SUBLANES - TOP_K
    e_ref[...] = jnp.concatenate(idxs + [jnp.zeros((pad_rows, tm), I32)], axis=0)
    w_ref[...] = jnp.concatenate([jnp.sum(jnp.where(h, top_w, 0.0), axis=0, keepdims=True) for h in hits]
                                 + [jnp.zeros((pad_rows, tm), F32)], axis=0)
    p_ref[...] = jnp.concatenate([jnp.sum(jnp.where(h, before, 0.0), axis=0, keepdims=True) for h in hits]
                                 + [jnp.zeros((pad_rows, tm), F32)], axis=0).astype(I32)


def route(logits, router_bias, n_e, tm):
    n, width = logits.shape
    assert n % tm == 0 and width == LANES and n_e <= LANES and TOP_K <= SUBLANES
    tok_t = pl.BlockSpec((SUBLANES, tm), lambda i: (0, i))
    col = pl.BlockSpec((n_e, 1), lambda i: (0, 0))
    return pl.pallas_call(
        functools.partial(_route_kernel, n_e=n_e),
        out_shape=[jax.ShapeDtypeStruct((SUBLANES, n), I32), jax.ShapeDtypeStruct((SUBLANES, n), F32),
                   jax.ShapeDtypeStruct((SUBLANES, n), I32), jax.ShapeDtypeStruct((n_e, 1), I32)],
        grid=(n // tm,),
        in_specs=[pl.BlockSpec((tm, LANES), lambda i: (i, 0)), col],
        out_specs=[tok_t, tok_t, tok_t, col],
        scratch_shapes=[pltpu.VMEM((n_e, 1), F32)],
        compiler_params=_cparams(("arbitrary",)),
        name="route",
    )(logits, router_bias.reshape(n_e, 1))


def _slot_rows_kernel(ps_ref, e_ref, p_ref, o_ref, *, n_e):
    e = e_ref[...]
    base = jnp.zeros(e.shape, I32)
    for x in range(n_e):
        base = jnp.where(e == x, ps_ref[x], base)
    o_ref[...] = base + p_ref[...]


def slot_rows(sel_e, sel_pos, pad_starts, tm):
    n = sel_e.shape[1]
    n_e = pad_starts.shape[0]
    tok_t = pl.BlockSpec((SUBLANES, tm), lambda i: (0, i))
    return pl.pallas_call(
        functools.partial(_slot_rows_kernel, n_e=n_e),
        out_shape=jax.ShapeDtypeStruct((SUBLANES, n), I32),
        grid=(n // tm,),
        in_specs=[pl.BlockSpec(memory_space=pltpu.SMEM), tok_t, tok_t],
        out_specs=tok_t,
        compiler_params=_cparams(("arbitrary",)),
        name="slot_rows",
    )(pad_starts.astype(I32), sel_e, sel_pos)


def _row_copy(src_ref, dst_ref, src_row, dst_row, sem):
    return pltpu.make_async_copy(src_ref.at[pl.ds(src_row, 1)], dst_ref.at[pl.ds(dst_row, 1)], sem)


def _dispatch_kernel(dest_ref, h_ref, hs_ref, sem):
    tm = h_ref.shape[0]

    def body(t, carry):
        for kk in range(TOP_K):
            _row_copy(h_ref, hs_ref, t, dest_ref[0, 0, t * TOP_K + kk], sem).start(priority=kk % 2)
        return carry

    def drain(t, carry):
        for kk in range(TOP_K):
            _row_copy(h_ref, hs_ref, t, dest_ref[0, 0, t * TOP_K + kk], sem).wait()
        return carry

    lax.fori_loop(0, tm, body, 0, unroll=4)
    lax.fori_loop(0, tm, drain, 0, unroll=4)


def dispatch(h_all, dest, n_rows, tm):
    n, d = h_all.shape
    assert n % tm == 0
    dest3 = dest.reshape(n // tm, 1, tm * TOP_K)
    return pl.pallas_call(
        _dispatch_kernel,
        out_shape=jax.ShapeDtypeStruct((n_rows, d), h_all.dtype),
        grid=(n // tm,),
        in_specs=[pl.BlockSpec((1, 1, tm * TOP_K), lambda i: (i, 0, 0), memory_space=pltpu.SMEM),
                  pl.BlockSpec((tm, d), lambda i: (i, 0))],
        out_specs=pl.BlockSpec(memory_space=pl.ANY),
        scratch_shapes=[pltpu.SemaphoreType.DMA(())],
        compiler_params=_cparams(("arbitrary",)),
        name="dispatch",
    )(dest3, h_all)


def _experts_kernel(be_ref, bf_ref, bv_ref, x_ref, wg_ref, wu_ref, wd_ref, y_ref, wg_sc, wu_sc, wd_sc):
    i = pl.program_id(0)
    valid = bv_ref[i]

    @pl.when(bf_ref[i] == 1)
    def _():
        wg_sc[...] = wg_ref[0].astype(BF16)
        wu_sc[...] = wu_ref[0].astype(BF16)
        wd_sc[...] = wd_ref[0].astype(BF16)

    @pl.when(valid > 0)
    def _():
        row = lax.broadcasted_iota(I32, x_ref.shape, 0)
        x = jnp.where(row < valid, x_ref[...], jnp.uint32(0))
        y_ref[...] = _pack_pairs(_swiglu_packed(x, wg_sc, wu_sc, wd_sc))

    @pl.when(valid == 0)
    def _():
        y_ref[...] = jnp.zeros_like(y_ref)


def _swiglu_packed(xp, wg_ref, wu_ref, wd_ref):
    lo, hi = _unpack_pairs(xp)
    lo = lo.astype(BF16)
    hi = hi.astype(BF16)
    half = xp.shape[-1]

    def proj(w_ref):
        return (jnp.dot(lo, w_ref[:half, :], preferred_element_type=F32)
                + jnp.dot(hi, w_ref[half:, :], preferred_element_type=F32))

    act = (_silu(proj(wg_ref)) * proj(wu_ref)).astype(BF16)
    return jnp.dot(act, wd_ref[...], preferred_element_type=F32)


def experts(h_sorted, blk_e, blk_first, blk_valid, w_gate, w_up, w_down):
    n_rows, dp = h_sorted.shape
    n_blocks = n_rows // EXPERT_ROWS
    d, de = w_gate.shape[-2:]
    assert d == 2 * dp
    grid_spec = pltpu.PrefetchScalarGridSpec(
        num_scalar_prefetch=3,
        grid=(n_blocks,),
        in_specs=[pl.BlockSpec((EXPERT_ROWS, dp), lambda i, be, bf, nu: (i, 0)),
                  pl.BlockSpec((1, d, de), lambda i, be, bf, nu: (be[i], 0, 0)),
                  pl.BlockSpec((1, d, de), lambda i, be, bf, nu: (be[i], 0, 0)),
                  pl.BlockSpec((1, de, d), lambda i, be, bf, nu: (be[i], 0, 0))],
        out_specs=pl.BlockSpec((EXPERT_ROWS, dp), lambda i, be, bf, nu: (i, 0)),
        scratch_shapes=[pltpu.VMEM((d, de), BF16), pltpu.VMEM((d, de), BF16), pltpu.VMEM((de, d), BF16)],
    )
    return pl.pallas_call(
        _experts_kernel,
        out_shape=jax.ShapeDtypeStruct((n_rows, dp), jnp.uint32),
        grid_spec=grid_spec,
        compiler_params=_cparams(("arbitrary",)),
        name="experts",
    )(blk_e, blk_first, blk_valid, h_sorted, w_gate, w_up, w_down)


def _combine_kernel(dest_ref, x1_ref, h2_ref, gf_ref, w_ref, wg_ref, wu_ref, wd_ref, nw_ref, yr_ref, o_ref,
                    gbuf, sem):
    bt, tl, d = x1_ref.shape
    tm = bt * tl

    def body(t, carry):
        for kk in range(TOP_K):
            _row_copy(yr_ref, gbuf.at[kk], dest_ref[0, 0, t * TOP_K + kk], t, sem).start(priority=kk % 2)
        return carry

    lax.fori_loop(0, tm, body, 0, unroll=4)
    y = _swiglu_packed(h2_ref[...].reshape(tm, d // 2), wg_ref, wu_ref, wd_ref)
    wts = w_ref[...]

    def drain(t, carry):
        for kk in range(TOP_K):
            _row_copy(yr_ref, gbuf.at[kk], dest_ref[0, 0, t * TOP_K + kk], t, sem).wait()
        return carry

    lax.fori_loop(0, tm, drain, 0)
    r_lo = jnp.zeros((tm, d // 2), F32)
    r_hi = jnp.zeros((tm, d // 2), F32)
    for kk in range(TOP_K):
        lo, hi = _unpack_pairs(gbuf[kk])
        r_lo = r_lo + lo * wts[:, kk:kk + 1]
        r_hi = r_hi + hi * wts[:, kk:kk + 1]
    y = y + jnp.concatenate([r_lo, r_hi], axis=1)
    x2 = x1_ref[...] + gf_ref[:, 0] * y.reshape(bt, tl, d)
    o_ref[...] = x2 * lax.rsqrt(jnp.mean(x2 * x2, axis=-1, keepdims=True) + EPS) * nw_ref[...]


def combine(x1, h2, mod4, sel_w, dest, y_rows, w_gs, w_us, w_ds, norm_final, bt, tl):
    b, l, d = x1.shape
    tm = bt * tl
    n = b * l
    ds = w_gs.shape[1]
    n_t = l // tl
    dest3 = dest.reshape(n // tm, 1, tm * TOP_K)

    def const(shape):
        return pl.BlockSpec(shape, lambda i, t: (0, 0), pipeline_mode=pl.Buffered(1))

    return pl.pallas_call(
        _combine_kernel,
        out_shape=jax.ShapeDtypeStruct((b, l, d), F32),
        grid=(b // bt, n_t),
        in_specs=[pl.BlockSpec((1, 1, tm * TOP_K), lambda i, t: (i * n_t + t, 0, 0), memory_space=pltpu.SMEM),
                  pl.BlockSpec((bt, tl, d), lambda i, t: (i, t, 0)),
                  pl.BlockSpec((bt, tl, d // 2), lambda i, t: (i, t, 0)),
                  pl.BlockSpec((bt, 1, 1, d), lambda i, t: (i, 5, 0, 0)),
                  pl.BlockSpec((tm, TOP_K), lambda i, t: (i * n_t + t, 0)),
                  const((d, ds)), const((d, ds)), const((ds, d)), const((1, d)),
                  pl.BlockSpec(memory_space=pl.ANY)],
        out_specs=pl.BlockSpec((bt, tl, d), lambda i, t: (i, t, 0)),
        scratch_shapes=[pltpu.VMEM((TOP_K, tm, d // 2), jnp.uint32), pltpu.SemaphoreType.DMA(())],
        compiler_params=_cparams(("arbitrary", "arbitrary")),
        name="combine",
    )(dest3, x1, h2, mod4, sel_w, w_gs, w_us, w_ds, norm_final.reshape(1, d), y_rows)


def _moe(h2_p, h2_s, x1_p, x1_s, lg_p, lg_s, mod_p, mod_s, router_bias, w_gate_e, w_up_e, w_down_e,
         w_gs, w_us, w_ds, norm_final, tiles_p, tiles_s):
    d = h2_p.shape[-1]
    n_p = h2_p.shape[0] * h2_p.shape[1]
    n_s = h2_s.shape[0] * h2_s.shape[1]
    n = n_p + n_s
    n_e = w_gate_e.shape[0]
    h_all = jnp.concatenate([h2_p.reshape(n_p, d), h2_s.reshape(n_s, d)], axis=0)
    logits = jnp.concatenate([lg_p.reshape(n_p, LANES), lg_s.reshape(n_s, LANES)], axis=0)
    sel_e, sel_w, sel_pos, counts = route(logits, router_bias, n_e, 512)
    counts = counts[:, 0]
    sel_w = sel_w[:TOP_K].T
    padded = (counts + EXPERT_ROWS - 1) // EXPERT_ROWS * EXPERT_ROWS
    pad_ends = jnp.cumsum(padded)
    pad_starts = pad_ends - padded
    n_blocks = -(-(n * TOP_K + n_e * (EXPERT_ROWS - 1)) // EXPERT_ROWS)
    n_rows = n_blocks * EXPERT_ROWS
    dest = slot_rows(sel_e, sel_pos, pad_starts, 512)[:TOP_K].T
    blk_start = jnp.arange(n_blocks, dtype=I32) * EXPERT_ROWS
    n_used = (pad_ends[-1] // EXPERT_ROWS).astype(I32)
    blk_e = jnp.minimum(jnp.sum((blk_start[:, None] >= pad_ends[None, :]).astype(I32), axis=1), n_e - 1)
    blk_valid = jnp.clip((pad_starts + counts)[blk_e] - blk_start, 0, EXPERT_ROWS).astype(I32)
    last_used_e = blk_e[jnp.maximum(n_used - 1, 0)]
    blk_e = jnp.where(jnp.arange(n_blocks) < n_used, blk_e, last_used_e)
    blk_first = jnp.concatenate([jnp.ones((1,), I32), (blk_e[1:] != blk_e[:-1]).astype(I32)])
    h_sorted = dispatch(h_all, dest, n_rows, 512)
    y_rows = experts(h_sorted, blk_e, blk_first, blk_valid, w_gate_e, w_up_e, w_down_e)
    y_p = combine(x1_p, h2_p, mod_p, sel_w[:n_p], dest[:n_p], y_rows, w_gs, w_us, w_ds, norm_final, *tiles_p)
    y_s = combine(x1_s, h2_s, mod_s, sel_w[n_p:], dest[n_p:], y_rows, w_gs, w_us, w_ds, norm_final, *tiles_s)
    return y_p, y_s


def kernel(x_prompt, x_sample, cache_k, cache_v, state_delta, state_conv, page_table, c_prompt, c_sample,
           w_ada, b_ada, norm_mix, w_in, conv_w, a_log, dt_bias, delta_norm, w_branch_a, w_branch_b, w_out,
           rel_bias, norm_ffn, router_w, router_bias, w_gate_e, w_up_e, w_down_e, w_gate_s, w_up_s, w_down_s,
           norm_final):
    depth = w_ada.shape[0]
    assert depth == 1, "the final norm is fused into the layer's last stage"
    n_b, seq, d = x_prompt.shape
    d_b, d_seq, _ = x_sample.shape
    n_pages = page_table.shape[1]
    page = cache_k.shape[2]
    past_len = n_pages * page
    a_width = A_HEADS * HEAD_DIM
    g_width = G_HEADS * HEAD_DIM
    conv_ch = 3 * g_width
    l = 0

    off_small = 3 * a_width + conv_ch + g_width
    w_l = w_in[l]
    w_main = jnp.concatenate([w_l[:, :off_small].astype(BF16), w_l[:, off_small + 2 * G_HEADS:].astype(BF16)],
                             axis=1)
    w_small = jnp.pad(w_l[:, off_small:off_small + 2 * G_HEADS], ((0, 0), (0, LANES - 2 * G_HEADS))).astype(BF16)
    assert w_main.shape[1] == IN_COL_TILES * IN_TILE

    c_all = jnp.concatenate([c_prompt, c_sample], axis=0)
    mod = ada_mod(c_all, w_ada[l], b_ada[l])
    mod_p = mod[:n_b].reshape(n_b, N_ADA, 1, d)
    mod_s = mod[n_b:].reshape(d_b, N_ADA, 1, d)

    tiles_p = (1, 512)
    tiles_s = (512 // d_seq, d_seq)
    post_p = (1, 256)
    post_s = (256 // d_seq, d_seq)
    wa = w_branch_a[l].astype(BF16)
    wb = w_branch_b[l].astype(BF16)
    wo = w_out[l].astype(BF16)
    rw = jnp.pad(router_w[l], ((0, 0), (0, LANES - router_w.shape[-1])))

    def mixer(x, mod4, tiles, post_tiles, attend, s0, buf0, chunk):
        a_q, a_k, a_v, g_qkv, g_z, gate_a, gate_b, small = in_proj(x, mod4, norm_mix[l], w_main, w_small, *tiles)
        y_a = attend(a_q, a_k, a_v)
        small_t = jnp.swapaxes(small[..., :2 * G_HEADS], 1, 2)
        y_b, s_new, tail = delta_rule(g_qkv, g_z, small, small_t, conv_w[l], a_log[l], dt_bias[l],
                                      delta_norm[l], s0, buf0, chunk)
        x1, h2, lg = post_mix(y_a, y_b, gate_a, gate_b, x, mod4, wa, wb, wo, norm_ffn[l], rw, *post_tiles)
        return x1, h2, lg, a_k, a_v, s_new, tail[:, SUBLANES - (CONV_W - 1):]

    s0_p = jnp.zeros((n_b, G_HEADS, HEAD_DIM, HEAD_DIM), F32)
    buf0_p = jnp.zeros((n_b, SUBLANES, conv_ch), F32)
    buf0_s = jnp.pad(state_conv[l], ((0, 0), (SUBLANES - (CONV_W - 1), 0), (0, 0)))

    x1_p, h2_p, lg_p, k_p, v_p, d_p, cv_p = mixer(
        x_prompt, mod_p, tiles_p, post_p, functools.partial(moba_prompt, rel_bias=rel_bias), s0_p, buf0_p,
        DELTA_CHUNK)
    x1_s, h2_s, lg_s, k_s, v_s, d_s, cv_s = mixer(
        x_sample, mod_s, tiles_s, post_s,
        functools.partial(moba_sample, cache_k=cache_k, cache_v=cache_v, layer=l, page_table=page_table,
                          rel_bias=rel_bias, past_len=past_len),
        state_delta[l], buf0_s, d_seq)

    comb_p = (1, 256)
    comb_s = (256 // d_seq, d_seq)
    y_p, y_s = _moe(h2_p, h2_s, x1_p, x1_s, lg_p, lg_s, mod_p, mod_s, router_bias[l],
                    w_gate_e[l], w_up_e[l], w_down_e[l],
                    w_gate_s[l].astype(BF16), w_up_s[l].astype(BF16), w_down_s[l].astype(BF16),
                    norm_final, comb_p, comb_s)

    def heads(t, n_heads):
        return t.reshape(t.shape[:-1] + (n_heads, HEAD_DIM))[None]

    return (y_p, y_s, heads(k_p, A_HEADS), heads(v_p, A_HEADS), heads(k_s, A_HEADS), heads(v_s, A_HEADS),
            d_p[None], cv_p[None], d_s[None], cv_s[None])
```
